```python
import math
import jax, jax.numpy as jnp
from jax import lax
import numpy as np

D_MODEL = 1024
BATCH = 8
SEQ = 2048
DEPTH = 4
DEC_BATCH = 128
DEC_SEQ = 1
PAST_LEN = 16384
PAGE_SIZE = 128

N_MIXERS = 2
N_A_LAYERS = (DEPTH + 1) // 2
N_B_LAYERS = DEPTH // 2
CHUNK = 128
D_A = 2 * D_MODEL
A_HEADS = 8
A_HEAD_DIM = D_A // A_HEADS
GROUP = 16
N_GROUPS = D_MODEL // GROUP
STATE = 64
LOG_DT_MIN = math.log(1e-3)
LOG_DT_MAX = math.log(1e-1)
N_MEM = 256
X_HEADS = 4
X_HEAD_DIM = D_MODEL // X_HEADS
D_FF = 2816
EPS = 1e-6

kernel_name = 'hybrid_gmlp_s5_macaron_memxattn_step'


def _rmsnorm(x, g):
    xf = x.astype(jnp.float32)
    y = xf * lax.rsqrt(jnp.mean(xf * xf, axis=-1, keepdims=True) + EPS)
    return (y * g.astype(jnp.float32)).astype(x.dtype)


def _layernorm(x, g, b):
    xf = x.astype(jnp.float32)
    mu = jnp.mean(xf, axis=-1, keepdims=True)
    xc = xf - mu
    var = jnp.mean(xc * xc, axis=-1, keepdims=True)
    y = xc * lax.rsqrt(var + EPS) * g.astype(jnp.float32) + b.astype(jnp.float32)
    return y.astype(x.dtype)


def _swiglu(h, wg, wu, wd):
    return (jax.nn.silu(h @ wg) * (h @ wu)) @ wd


def _cmul(ar, ai, br, bi):
    return ar * br - ai * bi, ar * bi + ai * br


def _chunk_mlp(h, w_in, ln_g, ln_b, w_s, b_s, w_out):
    bsz, length, _ = h.shape
    u, v = jnp.split(jax.nn.gelu(h @ w_in), 2, axis=-1)
    v = _layernorm(v, ln_g, ln_b)
    blk = length if length <= CHUNK else CHUNK
    n_chunks = -(-length // blk)
    pad = n_chunks * blk - length
    vc = jnp.pad(v, ((0, 0), (0, pad), (0, 0))).reshape(bsz, n_chunks, blk, A_HEADS, A_HEAD_DIM)
    ws = jnp.tril(w_s[:, :blk, :blk])
    bias = jnp.swapaxes(b_s[:, :blk], 0, 1)[:, :, None]
    mixed = jnp.einsum('hts,bcshd->bcthd', ws, vc) + bias
    mixed = mixed.reshape(bsz, n_chunks * blk, D_A)[:, :length]
    return (u * mixed) @ w_out, v


def _ssm_mixer(h, h0_re, h0_im, a_re, a_im, log_dt, b_re, b_im, c_re, c_im, d_skip, w_glu):
    f32 = jnp.float32
    bsz, length, _ = h.shape
    u = h.astype(f32).reshape(bsz, length, N_GROUPS, GROUP)
    a_re = a_re.astype(f32)
    a_im = a_im.astype(f32)
    dt = jnp.exp(log_dt.astype(f32))[:, None]
    mag = jnp.exp(a_re * dt)
    ab_re = mag * jnp.cos(a_im * dt)
    ab_im = mag * jnp.sin(a_im * dt)
    den = a_re * a_re + a_im * a_im
    nr = ab_re - 1.0
    ni = ab_im
    f_re = (nr * a_re + ni * a_im) / den
    f_im = (ni * a_re - nr * a_im) / den
    bb_re, bb_im = _cmul(f_re[..., None], f_im[..., None], b_re.astype(f32), b_im.astype(f32))
    bu_re = jnp.einsum('gpc,blgc->blgp', bb_re, u)
    bu_im = jnp.einsum('gpc,blgc->blgp', bb_im, u)
    shape = (1, length, N_GROUPS, STATE)
    elems = (jnp.broadcast_to(ab_re, shape), jnp.broadcast_to(ab_im, shape), bu_re, bu_im)

    def combine(e1, e2):
        a1r, a1i, b1r, b1i = e1
        a2r, a2i, b2r, b2i = e2
        ar, ai = _cmul(a2r, a2i, a1r, a1i)
        br, bi = _cmul(a2r, a2i, b1r, b1i)
        return ar, ai, br + b2r, bi + b2i

    acr, aci, bcr, bci = lax.associative_scan(combine, elems, axis=1)
    pr, pi = _cmul(acr, aci, h0_re.astype(f32)[:, None], h0_im.astype(f32)[:, None])
    hr = pr + bcr
    hi = pi + bci
    c_re = c_re.astype(f32)
    c_im = c_im.astype(f32)
    y = (jnp.einsum('gcp,blgp->blgc', c_re, hr) - jnp.einsum('gcp,blgp->blgc', c_im, hi)
         + d_skip.astype(f32) * u)
    z = jax.nn.gelu(y.reshape(bsz, length, D_MODEL)).astype(h.dtype)
    val, gate = jnp.split(z @ w_glu, 2, axis=-1)
    return val * jax.nn.sigmoid(gate), hr[:, -1], hi[:, -1]


def _mem_kv(mem, g, w_k, w_v):
    bsz = mem.shape[0]
    m = _rmsnorm(mem, g)
    k = (m @ w_k).reshape(bsz, N_MEM, X_HEADS, X_HEAD_DIM)
    v = (m @ w_v).reshape(bsz, N_MEM, X_HEADS, X_HEAD_DIM)
    return k, v


def _cross_attn(h, k, v, w_q, w_o):
    bsz, length, _ = h.shape
    q = (h @ w_q).reshape(bsz, length, X_HEADS, X_HEAD_DIM)
    s = jnp.einsum('blhd,bmhd->bhlm', q, k).astype(jnp.float32) * (X_HEAD_DIM ** -0.5)
    p = jax.nn.softmax(s, axis=-1).astype(v.dtype)
    o = jnp.einsum('bhlm,bmhd->blhd', p, v).reshape(bsz, length, D_MODEL)
    return o @ w_o


def _trunk(x, mem_k, mem_v, h0_re, h0_im, w):
    (norm_ffn1, ffn1_wg, ffn1_wu, ffn1_wd, norm_mix,
     a_w_in, a_ln_g, a_ln_b, a_w_s, a_b_s, a_w_out,
     b_a_re, b_a_im, b_log_dt, b_b_re, b_b_im, b_c_re, b_c_im, b_d, b_w_glu,
     norm_x, x_wq, x_wo,
     norm_ffn2, ffn2_wg, ffn2_wu, ffn2_wd, norm_final) = w
    new_re, new_im, new_v = [], [], []
    for i in range(DEPTH):
        x = x + 0.5 * _swiglu(_rmsnorm(x, norm_ffn1[i]), ffn1_wg[i], ffn1_wu[i], ffn1_wd[i])
        h = _rmsnorm(x, norm_mix[i])
        j = i // N_MIXERS
        if i % N_MIXERS == 0:
            out, v_rows = _chunk_mlp(h, a_w_in[j], a_ln_g[j], a_ln_b[j], a_w_s[j], a_b_s[j], a_w_out[j])
            new_v.append(v_rows)
        else:
            out, hr, hi = _ssm_mixer(h, h0_re[j], h0_im[j], b_a_re[j], b_a_im[j], b_log_dt[j],
                                     b_b_re[j], b_b_im[j], b_c_re[j], b_c_im[j], b_d[j], b_w_glu[j])
            new_re.append(hr)
            new_im.append(hi)
        x = x + out
        x = x + _cross_attn(_rmsnorm(x, norm_x[i]), mem_k[i], mem_v[i], x_wq[i], x_wo[i])
        x = x + 0.5 * _swiglu(_rmsnorm(x, norm_ffn2[i]), ffn2_wg[i], ffn2_wu[i], ffn2_wd[i])
    return _rmsnorm(x, norm_final), jnp.stack(new_re), jnp.stack(new_im), jnp.stack(new_v)


def setup_inputs(seed: int = 0) -> dict:
    key = jax.random.key(seed)
    ks = iter(jax.random.split(key, 64))
    f32 = jnp.float32

    def nrm(shape, scale):
        return jax.random.normal(next(ks), shape, f32) * scale

    def gain(shape):
        return 1.0 + nrm(shape, 0.05)

    d = D_MODEL
    inp = {}
    inp['x_prompt'] = nrm((BATCH, SEQ, d), 1.0)
    inp['x_sample'] = nrm((DEC_BATCH, DEC_SEQ, d), 1.0)
    inp['cache_mem_k'] = nrm((DEPTH, DEC_BATCH, N_MEM, X_HEADS, X_HEAD_DIM), 1.0)
    inp['cache_mem_v'] = nrm((DEPTH, DEC_BATCH, N_MEM, X_HEADS, X_HEAD_DIM), 1.0)
    inp['state_ssm_re'] = nrm((N_B_LAYERS, DEC_BATCH, N_GROUPS, STATE), 0.5)
    inp['state_ssm_im'] = nrm((N_B_LAYERS, DEC_BATCH, N_GROUPS, STATE), 0.5)
    inp['mem_prompt'] = nrm((BATCH, N_MEM, d), 1.0)
    inp['norm_ffn1'] = gain((DEPTH, d))
    inp['ffn1_wg'] = nrm((DEPTH, d, D_FF), d ** -0.5)
    inp['ffn1_wu'] = nrm((DEPTH, d, D_FF), d ** -0.5)
    inp['ffn1_wd'] = nrm((DEPTH, D_FF, d), D_FF ** -0.5)
    inp['norm_mix'] = gain((DEPTH, d))
    inp['a_w_in'] = nrm((N_A_LAYERS, d, 2 * D_A), d ** -0.5)
    inp['a_ln_g'] = gain((N_A_LAYERS, D_A))
    inp['a_ln_b'] = nrm((N_A_LAYERS, D_A), 0.02)
    inp['a_w_s'] = nrm((N_A_LAYERS, A_HEADS, CHUNK, CHUNK), CHUNK ** -0.5)
    inp['a_b_s'] = 1.0 + nrm((N_A_LAYERS, A_HEADS, CHUNK), 0.1)
    inp['a_w_out'] = nrm((N_A_LAYERS, D_A, d), D_A ** -0.5)
    n_idx = jnp.arange(STATE, dtype=f32)
    inp['b_a_re'] = -0.5 + nrm((N_B_LAYERS, N_GROUPS, STATE), 0.01)
    inp['b_a_im'] = math.pi * n_idx + nrm((N_B_LAYERS, N_GROUPS, STATE), 0.01)
    inp['b_log_dt'] = jax.random.uniform(next(ks), (N_B_LAYERS, N_GROUPS), f32, LOG_DT_MIN, LOG_DT_MAX)
    inp['b_b_re'] = nrm((N_B_LAYERS, N_GROUPS, STATE, GROUP), (2.0 * GROUP) ** -0.5)
    inp['b_b_im'] = nrm((N_B_LAYERS, N_GROUPS, STATE, GROUP), (2.0 * GROUP) ** -0.5)
    inp['b_c_re'] = nrm((N_B_LAYERS, N_GROUPS, GROUP, STATE), (2.0 * STATE) ** -0.5)
    inp['b_c_im'] = nrm((N_B_LAYERS, N_GROUPS, GROUP, STATE), (2.0 * STATE) ** -0.5)
    inp['b_d'] = nrm((N_B_LAYERS, N_GROUPS, GROUP), 1.0)
    inp['b_w_glu'] = nrm((N_B_LAYERS, d, 2 * d), d ** -0.5)
    inp['norm_x'] = gain((DEPTH, d))
    inp['norm_mem'] = gain((DEPTH, d))
    inp['x_wq'] = nrm((DEPTH, d, d), d ** -0.5)
    inp['x_wk'] = nrm((DEPTH, d, d), d ** -0.5)
    inp['x_wv'] = nrm((DEPTH, d, d), d ** -0.5)
    inp['x_wo'] = nrm((DEPTH, d, d), d ** -0.5)
    inp['norm_ffn2'] = gain((DEPTH, d))
    inp['ffn2_wg'] = nrm((DEPTH, d, D_FF), d ** -0.5)
    inp['ffn2_wu'] = nrm((DEPTH, d, D_FF), d ** -0.5)
    inp['ffn2_wd'] = nrm((DEPTH, D_FF, d), D_FF ** -0.5)
    inp['norm_final'] = gain((d,))
    return inp


def reference(x_prompt, x_sample, cache_mem_k, cache_mem_v, state_ssm_re, state_ssm_im, mem_prompt,
              norm_ffn1, ffn1_wg, ffn1_wu, ffn1_wd, norm_mix,
              a_w_in, a_ln_g, a_ln_b, a_w_s, a_b_s, a_w_out,
              b_a_re, b_a_im, b_log_dt, b_b_re, b_b_im, b_c_re, b_c_im, b_d, b_w_glu,
              norm_x, norm_mem, x_wq, x_wk, x_wv, x_wo,
              norm_ffn2, ffn2_wg, ffn2_wu, ffn2_wd, norm_final):
    w = (norm_ffn1, ffn1_wg, ffn1_wu, ffn1_wd, norm_mix,
         a_w_in, a_ln_g, a_ln_b, a_w_s, a_b_s, a_w_out,
         b_a_re, b_a_im, b_log_dt, b_b_re, b_b_im, b_c_re, b_c_im, b_d, b_w_glu,
         norm_x, x_wq, x_wo,
         norm_ffn2, ffn2_wg, ffn2_wu, ffn2_wd, norm_final)

    kv = [_mem_kv(mem_prompt, norm_mem[i], x_wk[i], x_wv[i]) for i in range(DEPTH)]
    mem_k_prompt = jnp.stack([k for k, _ in kv])
    mem_v_prompt = jnp.stack([v for _, v in kv])
    h0 = jnp.zeros((N_B_LAYERS, x_prompt.shape[0], N_GROUPS, STATE), jnp.float32)
    y_prompt, ssm_re_prompt, ssm_im_prompt, _ = _trunk(x_prompt, mem_k_prompt, mem_v_prompt, h0, h0, w)

    y_sample, ssm_re_sample, ssm_im_sample, chunk_v_sample = _trunk(
        x_sample, cache_mem_k, cache_mem_v, state_ssm_re, state_ssm_im, w)

    return (y_prompt, y_sample, mem_k_prompt, mem_v_prompt, ssm_re_prompt, ssm_im_prompt,
            ssm_re_sample, ssm_im_sample, chunk_v_sample)
```

```python
import functools
import math

import jax
import jax.numpy as jnp
from jax import lax
from jax.experimental import pallas as pl
from jax.experimental.pallas import tpu as pltpu

F32 = jnp.float32
BF16 = jnp.bfloat16

D_MODEL = 1024
DEPTH = 4
CHUNK = 128
D_A = 2 * D_MODEL
A_HEADS = 8
A_HEAD_DIM = D_A // A_HEADS
GROUP = 16
N_GROUPS = D_MODEL // GROUP
STATE = 64
SSM_LANES = N_GROUPS * STATE
N_MEM = 256
X_HEADS = 4
X_HEAD_DIM = D_MODEL // X_HEADS
D_FF = 2816
EPS = 1e-6

V7X_SUBLANES = 8
V7X_MXU_DIM = 256
V7X_VMEM_LIMIT_BYTES = 56 * 1024 * 1024

ROW_TILE = 512
SSM_TIME_TILE = 64
SSM_GROUP_BLOCK = 16
SSM_BLOCKS = N_GROUPS // SSM_GROUP_BLOCK
SCAN_LANES = 512
SAMPLE_ATTN_BATCH = 8


def _params(*semantics):
    return pltpu.CompilerParams(dimension_semantics=semantics,
                                vmem_limit_bytes=V7X_VMEM_LIMIT_BYTES)


def _const_spec(shape):
    nd = len(shape)
    return pl.BlockSpec(shape, lambda *_: (0,) * nd, pipeline_mode=pl.Buffered(1))


def _layer_spec(shape, layer):
    nd = len(shape)
    return pl.BlockSpec((None,) + tuple(shape), lambda *_: (layer,) + (0,) * nd,
                        pipeline_mode=pl.Buffered(1))


def _rmsnorm(x, g):
    return x * lax.rsqrt(jnp.mean(x * x, axis=-1, keepdims=True) + EPS) * g


def _dot(a, b):
    return jnp.dot(a, b, preferred_element_type=F32)


def _ffn_kernel(x_ref, g_ref, wg_ref, wu_ref, wd_ref, gf_ref, o_ref, *, final_norm):
    x = x_ref[...]
    h = _rmsnorm(x, g_ref[...]).astype(BF16)
    gate = _dot(h, wg_ref[...])
    up = _dot(h, wu_ref[...])
    act = (jax.nn.silu(gate) * up).astype(BF16)
    y = x + 0.5 * _dot(act, wd_ref[...])
    if final_norm:
        y = _rmsnorm(y, gf_ref[...])
    o_ref[...] = y


def _ffn(x, g, wg, wu, wd, g_final, layer, final_norm):
    rows = x.shape[0]
    tm = min(ROW_TILE, rows)
    row_spec = pl.BlockSpec((tm, D_MODEL), lambda i: (i, 0))
    return pl.pallas_call(
        functools.partial(_ffn_kernel, final_norm=final_norm),
        out_shape=jax.ShapeDtypeStruct(x.shape, F32),
        grid=(rows // tm,),
        in_specs=[row_spec,
                  _layer_spec((1, D_MODEL), layer),
                  _layer_spec((D_MODEL, D_FF), layer),
                  _layer_spec((D_MODEL, D_FF), layer),
                  _layer_spec((D_FF, D_MODEL), layer),
                  _const_spec((1, D_MODEL))],
        out_specs=row_spec,
        compiler_params=_params("parallel"),
        name="ffn",
    )(x, g, wg, wu, wd, g_final)


def _gated_unit_inputs(x_ref, g_ref, w_in_ref, ln_g_ref, ln_b_ref):
    h = _rmsnorm(x_ref[...], g_ref[...]).astype(BF16)
    uv = jax.nn.gelu(_dot(h, w_in_ref[...]))
    u = uv[:, :D_A]
    v = uv[:, D_A:]
    mu = jnp.mean(v, axis=-1, keepdims=True)
    vc = v - mu
    var = jnp.mean(vc * vc, axis=-1, keepdims=True)
    v = vc * lax.rsqrt(var + EPS) * ln_g_ref[...] + ln_b_ref[...]
    return u, v


def _mixer_a_seq_kernel(x_ref, g_ref, w_in_ref, ln_g_ref, ln_b_ref, ws_ref, bias_ref, w_out_ref,
                        o_ref, gated_ref):
    u, v = _gated_unit_inputs(x_ref, g_ref, w_in_ref, ln_g_ref, ln_b_ref)
    vb = v.astype(BF16)
    t_idx = lax.broadcasted_iota(jnp.int32, (CHUNK, CHUNK), 0)
    s_idx = lax.broadcasted_iota(jnp.int32, (CHUNK, CHUNK), 1)
    causal = t_idx >= s_idx
    for head in range(A_HEADS):
        cols = slice(head * A_HEAD_DIM, (head + 1) * A_HEAD_DIM)
        ws = jnp.where(causal, ws_ref[head], 0.0).astype(BF16)
        bias = bias_ref[:, cols]
        for c in range(x_ref.shape[0] // CHUNK):
            rows = slice(c * CHUNK, (c + 1) * CHUNK)
            mixed = _dot(ws, vb[rows, cols]) + bias
            gated_ref[rows, cols] = (u[rows, cols] * mixed).astype(BF16)
    o_ref[...] = x_ref[...] + _dot(gated_ref[...], w_out_ref[...])


def _mixer_a_step_kernel(x_ref, g_ref, w_in_ref, ln_g_ref, ln_b_ref, scale_ref, bias_ref, w_out_ref,
                         o_ref, v_ref):
    u, v = _gated_unit_inputs(x_ref, g_ref, w_in_ref, ln_g_ref, ln_b_ref)
    v_ref[...] = v
    mixed = v * scale_ref[...] + bias_ref[...]
    o_ref[...] = x_ref[...] + _dot((u * mixed).astype(BF16), w_out_ref[...])


def _mixer_a_seq(x, g, w_in, ln_g, ln_b, w_s, bias_full, w_out, layer, j):
    rows = x.shape[0]
    tm = ROW_TILE
    row_spec = pl.BlockSpec((tm, D_MODEL), lambda i: (i, 0))
    return pl.pallas_call(
        _mixer_a_seq_kernel,
        out_shape=jax.ShapeDtypeStruct(x.shape, F32),
        grid=(rows // tm,),
        in_specs=[row_spec,
                  _layer_spec((1, D_MODEL), layer),
                  _layer_spec((D_MODEL, 2 * D_A), j),
                  _layer_spec((1, D_A), j),
                  _layer_spec((1, D_A), j),
                  _layer_spec((A_HEADS, CHUNK, CHUNK), j),
                  _layer_spec((CHUNK, D_A), j),
                  _layer_spec((D_A, D_MODEL), j)],
        out_specs=row_spec,
        scratch_shapes=[pltpu.VMEM((tm, D_A), BF16)],
        compiler_params=_params("parallel"),
        name="mixer_a_seq",
    )(x, g, w_in, ln_g, ln_b, w_s, bias_full, w_out)


def _mixer_a_step(x, g, w_in, ln_g, ln_b, scale, bias, w_out, layer, j):
    rows = x.shape[0]
    return pl.pallas_call(
        _mixer_a_step_kernel,
        out_shape=(jax.ShapeDtypeStruct(x.shape, F32), jax.ShapeDtypeStruct((rows, D_A), F32)),
        grid=(1,),
        in_specs=[_const_spec((rows, D_MODEL)),
                  _layer_spec((1, D_MODEL), layer),
                  _layer_spec((D_MODEL, 2 * D_A), j),
                  _layer_spec((1, D_A), j),
                  _layer_spec((1, D_A), j),
                  _layer_spec((1, D_A), j),
                  _layer_spec((1, D_A), j),
                  _layer_spec((D_A, D_MODEL), j)],
        out_specs=(_const_spec((rows, D_MODEL)), _const_spec((rows, D_A))),
        compiler_params=_params("arbitrary"),
        name="mixer_a_step",
    )(x, g, w_in, ln_g, ln_b, scale, bias, w_out)


def _ssm_discretize_kernel(a_re_ref, a_im_ref, log_dt_ref, b_re_ref, b_im_ref,
                           ab_re_ref, ab_im_ref, bb_re_ref, bb_im_ref):
    a_re = a_re_ref[...]
    a_im = a_im_ref[...]
    dt = jnp.exp(log_dt_ref[...])
    mag = jnp.exp(a_re * dt)
    ab_re = mag * jnp.cos(a_im * dt)
    ab_im = mag * jnp.sin(a_im * dt)
    den = a_re * a_re + a_im * a_im
    nr = ab_re - 1.0
    ni = ab_im
    f_re = (nr * a_re + ni * a_im) / den
    f_im = (ni * a_re - nr * a_im) / den
    b_re = b_re_ref[...]
    b_im = b_im_ref[...]
    ab_re_ref[...] = ab_re
    ab_im_ref[...] = ab_im
    bb_re_ref[...] = f_re * b_re - f_im * b_im
    bb_im_ref[...] = f_re * b_im + f_im * b_re


def _ssm_discretize(a_re, a_im, log_dt, b_re_t, b_im_t):
    n_layers = a_re.shape[0]
    ab_shape = jax.ShapeDtypeStruct((n_layers, N_GROUPS, 1, STATE), F32)
    bb_shape = jax.ShapeDtypeStruct((n_layers, N_GROUPS, GROUP, STATE), F32)
    return pl.pallas_call(
        _ssm_discretize_kernel,
        out_shape=(ab_shape, ab_shape, bb_shape, bb_shape),
        name="ssm_discretize",
    )(a_re.reshape(ab_shape.shape), a_im.reshape(ab_shape.shape),
      log_dt.reshape(n_layers, N_GROUPS, 1, 1), b_re_t, b_im_t)


def _ssm_kernel(x_ref, g_ref, wb_ref, ab_re_ref, ab_im_ref, wc_re_ref, wc_im_ref, d_ref, wglu_ref,
                h0_re_ref, h0_im_ref, o_ref, hf_re_ref, hf_im_ref,
                s_re_ref, s_im_ref, st_re_ref, st_im_ref, *, n_batch, n_time):
    step = pl.program_id(0)
    blk = SSM_GROUP_BLOCK * STATE
    cblk = SSM_GROUP_BLOCK * GROUP

    @pl.when(step == 0)
    def _():
        st_re_ref[...] = h0_re_ref[...]
        st_im_ref[...] = h0_im_ref[...]

    x = x_ref[...]
    u = _rmsnorm(x, g_ref[...])
    ub = u.astype(BF16)
    for k in range(SSM_BLOCKS):
        bu = _dot(ub[:, k * cblk:(k + 1) * cblk], wb_ref[k])
        s_re_ref[:, k * blk:(k + 1) * blk] = bu[:, :blk]
        s_im_ref[:, k * blk:(k + 1) * blk] = bu[:, blk:]

    if n_time == 1:
        ar = ab_re_ref[...]
        ai = ab_im_ref[...]
        hr0 = st_re_ref[...]
        hi0 = st_im_ref[...]
        hr = ar * hr0 - ai * hi0 + s_re_ref[...]
        hi = ar * hi0 + ai * hr0 + s_im_ref[...]
        s_re_ref[...] = hr
        s_im_ref[...] = hi
        st_re_ref[...] = hr
        st_im_ref[...] = hi
    else:
        for c in range(SSM_LANES // SCAN_LANES):
            lanes = slice(c * SCAN_LANES, (c + 1) * SCAN_LANES)
            ar = jnp.broadcast_to(ab_re_ref[:, lanes], (V7X_SUBLANES, SCAN_LANES))
            ai = jnp.broadcast_to(ab_im_ref[:, lanes], (V7X_SUBLANES, SCAN_LANES))
            for s in range(n_batch // V7X_SUBLANES):
                rows0 = slice(s * V7X_SUBLANES, (s + 1) * V7X_SUBLANES)

                def body(t, carry, s=s, lanes=lanes, ar=ar, ai=ai):
                    hr, hi = carry
                    row = pl.multiple_of(t * n_batch + s * V7X_SUBLANES, V7X_SUBLANES)
                    rows = pl.ds(row, V7X_SUBLANES)
                    nr = ar * hr - ai * hi + s_re_ref[rows, lanes]
                    ni = ar * hi + ai * hr + s_im_ref[rows, lanes]
                    s_re_ref[rows, lanes] = nr
                    s_im_ref[rows, lanes] = ni
                    return nr, ni

                hr, hi = lax.fori_loop(0, n_time, body,
                                       (st_re_ref[rows0, lanes], st_im_ref[rows0, lanes]), unroll=2)
                st_re_ref[rows0, lanes] = hr
                st_im_ref[rows0, lanes] = hi

    ys = []
    for k in range(SSM_BLOCKS):
        hr = s_re_ref[:, k * blk:(k + 1) * blk].astype(BF16)
        hi = s_im_ref[:, k * blk:(k + 1) * blk].astype(BF16)
        ys.append(_dot(hr, wc_re_ref[k]) - _dot(hi, wc_im_ref[k]))
    y = jnp.concatenate(ys, axis=-1) + d_ref[...] * u
    z = jax.nn.gelu(y).astype(BF16)
    vg = _dot(z, wglu_ref[...])
    o_ref[...] = x + vg[:, :D_MODEL] * jax.nn.sigmoid(vg[:, D_MODEL:])

    @pl.when(step == pl.num_programs(0) - 1)
    def _():
        hf_re_ref[...] = st_re_ref[...]
        hf_im_ref[...] = st_im_ref[...]


def _ssm_mixer(x_tm, g, wb, ab_re, ab_im, wc_re, wc_im, d_skip, w_glu, h0_re, h0_im,
               layer, j, n_batch, n_time_total):
    n_time = min(SSM_TIME_TILE, n_time_total)
    tile = n_time * n_batch
    row_spec = pl.BlockSpec((tile, D_MODEL), lambda i: (i, 0))
    state_spec = _const_spec((n_batch, SSM_LANES))
    state_shape = jax.ShapeDtypeStruct((n_batch, SSM_LANES), F32)
    blk = SSM_GROUP_BLOCK * STATE
    cblk = SSM_GROUP_BLOCK * GROUP
    return pl.pallas_call(
        functools.partial(_ssm_kernel, n_batch=n_batch, n_time=n_time),
        out_shape=(jax.ShapeDtypeStruct(x_tm.shape, F32), state_shape, state_shape),
        grid=(n_time_total // n_time,),
        in_specs=[row_spec,
                  _layer_spec((1, D_MODEL), layer),
                  _layer_spec((SSM_BLOCKS, cblk, 2 * blk), j),
                  _layer_spec((1, SSM_LANES), j),
                  _layer_spec((1, SSM_LANES), j),
                  _layer_spec((SSM_BLOCKS, blk, cblk), j),
                  _layer_spec((SSM_BLOCKS, blk, cblk), j),
                  _layer_spec((1, D_MODEL), j),
                  _layer_spec((D_MODEL, 2 * D_MODEL), j),
                  state_spec, state_spec],
        out_specs=(row_spec, state_spec, state_spec),
        scratch_shapes=[pltpu.VMEM((tile, SSM_LANES), F32), pltpu.VMEM((tile, SSM_LANES), F32),
                        pltpu.VMEM((n_batch, SSM_LANES), F32), pltpu.VMEM((n_batch, SSM_LANES), F32)],
        compiler_params=_params("arbitrary"),
        name="ssm_mixer",
    )(x_tm, g, wb, ab_re, ab_im, wc_re, wc_im, d_skip, w_glu, h0_re, h0_im)


def _block_diag(w, rows_inner, cols_inner):
    n_layers = w.shape[0]
    gb = SSM_GROUP_BLOCK
    w = w.reshape(n_layers, SSM_BLOCKS, gb, rows_inner, cols_inner)
    eye = jnp.eye(gb, dtype=w.dtype)
    out = jnp.einsum('lkgrc,gh->lkgrhc', w, eye)
    return out.reshape(n_layers, SSM_BLOCKS, gb * rows_inner, gb * cols_inner)


def _mem_kv_kernel(m_ref, g_ref, wk_ref, wv_ref, k_ref, v_ref):
    m = _rmsnorm(m_ref[...], g_ref[...]).astype(BF16)
    k_ref[...] = _dot(m, wk_ref[...])
    v_ref[...] = _dot(m, wv_ref[...])


def _mem_kv(mem, g, wk, wv):
    rows = mem.shape[0]
    tm = ROW_TILE
    out = jax.ShapeDtypeStruct((DEPTH, rows, D_MODEL), F32)
    w_spec = pl.BlockSpec((None, D_MODEL, D_MODEL), lambda l, i: (l, 0, 0))
    o_spec = pl.BlockSpec((None, tm, D_MODEL), lambda l, i: (l, i, 0))
    return pl.pallas_call(
        _mem_kv_kernel,
        out_shape=(out, out),
        grid=(DEPTH, rows // tm),
        in_specs=[pl.BlockSpec((tm, D_MODEL), lambda l, i: (i, 0)),
                  pl.BlockSpec((None, 1, D_MODEL), lambda l, i: (l, 0, 0)),
                  w_spec, w_spec],
        out_specs=(o_spec, o_spec),
        compiler_params=_params("parallel", "parallel"),
        name="mem_kv",
    )(mem, g, wk, wv)


def _softmax_rows(s):
    e = jnp.exp(s - jnp.max(s, axis=-1, keepdims=True))
    return e / jnp.sum(e, axis=-1, keepdims=True)


def _attn_seq_kernel(x_ref, g_ref, wq_ref, k_ref, v_ref, wo_ref, o_ref, oh_ref):
    x = x_ref[...]
    h = _rmsnorm(x, g_ref[...]).astype(BF16)
    q = _dot(h, wq_ref[...])
    kb = k_ref[...].astype(BF16)
    vb = v_ref[...].astype(BF16)
    scale = X_HEAD_DIM ** -0.5
    for head in range(X_HEADS):
        cols = slice(head * X_HEAD_DIM, (head + 1) * X_HEAD_DIM)
        s = lax.dot_general(q[:, cols].astype(BF16), kb[:, cols], (((1,), (1,)), ((), ())),
                            preferred_element_type=F32) * scale
        p = _softmax_rows(s).astype(BF16)
        oh_ref[:, cols] = _dot(p, vb[:, cols]).astype(BF16)
    o_ref[...] = x + _dot(oh_ref[...], wo_ref[...])


def _attn_seq(x, g, wq, k, v, wo, layer, n_seq, seq_len, time_major):
    tm = ROW_TILE
    n_t = seq_len // tm
    if time_major:
        x2 = x.reshape(seq_len, n_seq * D_MODEL)
        row_spec = pl.BlockSpec((tm, D_MODEL), lambda b, i: (i, b))
    else:
        x2 = x
        row_spec = pl.BlockSpec((tm, D_MODEL), lambda b, i: (b * n_t + i, 0))
    kv_spec = pl.BlockSpec((None, None, N_MEM, D_MODEL), lambda b, i: (layer, b, 0, 0))
    out = pl.pallas_call(
        _attn_seq_kernel,
        out_shape=jax.ShapeDtypeStruct(x2.shape, F32),
        grid=(n_seq, n_t),
        in_specs=[row_spec,
                  _layer_spec((1, D_MODEL), layer),
                  _layer_spec((D_MODEL, D_MODEL), layer),
                  kv_spec, kv_spec,
                  _layer_spec((D_MODEL, D_MODEL), layer)],
        out_specs=row_spec,
        scratch_shapes=[pltpu.VMEM((tm, D_MODEL), BF16)],
        compiler_params=_params("parallel", "parallel"),
        name="attn_seq",
    )(x2, g, wq, k, v, wo)
    return out.reshape(x.shape)


def _attn_step_kernel(x_ref, g_ref, wq_ref, k_ref, v_ref, wo_ref, o_ref, q_ref, oh_ref):
    step = pl.program_id(0)
    nb = SAMPLE_ATTN_BATCH

    @pl.when(step == 0)
    def _():
        h = _rmsnorm(x_ref[...], g_ref[...]).astype(BF16)
        q_ref[...] = _dot(h, wq_ref[...])

    row0 = pl.multiple_of(step * nb, nb)
    q = q_ref[pl.ds(row0, nb), :]
    head_of_lane = lax.broadcasted_iota(jnp.int32, (V7X_SUBLANES, D_MODEL), 1) // X_HEAD_DIM
    head_of_row = lax.broadcasted_iota(jnp.int32, (V7X_SUBLANES, D_MODEL), 0)
    own_head = head_of_lane == head_of_row
    scale = X_HEAD_DIM ** -0.5
    outs = []
    for s in range(nb):
        qh = jnp.where(own_head, q[s:s + 1, :], 0.0).astype(BF16)
        sc = lax.dot_general(qh, k_ref[s].astype(BF16), (((1,), (1,)), ((), ())),
                             preferred_element_type=F32) * scale
        p = _softmax_rows(sc).astype(BF16)
        pv = _dot(p, v_ref[s].astype(BF16))
        outs.append(jnp.sum(jnp.where(own_head, pv, 0.0), axis=0, keepdims=True))
    oh_ref[pl.ds(row0, nb), :] = jnp.concatenate(outs, axis=0)

    @pl.when(step == pl.num_programs(0) - 1)
    def _():
        o_ref[...] = x_ref[...] + _dot(oh_ref[...].astype(BF16), wo_ref[...])


def _attn_step(x, g, wq, k, v, wo, layer):
    rows = x.shape[0]
    nb = SAMPLE_ATTN_BATCH
    kv_spec = pl.BlockSpec((None, nb, N_MEM, D_MODEL), lambda i: (layer, i, 0, 0))
    return pl.pallas_call(
        _attn_step_kernel,
        out_shape=jax.ShapeDtypeStruct(x.shape, F32),
        grid=(rows // nb,),
        in_specs=[_const_spec((rows, D_MODEL)),
                  _layer_spec((1, D_MODEL), layer),
                  _layer_spec((D_MODEL, D_MODEL), layer),
                  kv_spec, kv_spec,
                  _layer_spec((D_MODEL, D_MODEL), layer)],
        out_specs=_const_spec((rows, D_MODEL)),
        scratch_shapes=[pltpu.VMEM((rows, D_MODEL), F32), pltpu.VMEM((rows, D_MODEL), F32)],
        compiler_params=_params("arbitrary"),
        name="attn_step",
    )(x, g, wq, k, v, wo)


def _trunk(x, n_seq, seq_len, mem_k, mem_v, h0_re, h0_im, w):
    new_re, new_im, new_v = [], [], []
    for i in range(DEPTH):
        j = i // 2
        x = _ffn(x, w['norm_ffn1'], w['ffn1_wg'], w['ffn1_wu'], w['ffn1_wd'], w['norm_final'], i, False)
        time_major = False
        if i % 2 == 0:
            if seq_len == 1:
                x, v_rows = _mixer_a_step(x, w['norm_mix'], w['a_w_in'], w['a_ln_g'], w['a_ln_b'],
                                          w['a_scale0'], w['a_bias0'], w['a_w_out'], i, j)
                new_v.append(v_rows)
            else:
                x = _mixer_a_seq(x, w['norm_mix'], w['a_w_in'], w['a_ln_g'], w['a_ln_b'],
                                 w['a_w_s'], w['a_bias_full'], w['a_w_out'], i, j)
        else:
            if seq_len > 1:
                x = x.reshape(n_seq, seq_len, D_MODEL).transpose(1, 0, 2).reshape(-1, D_MODEL)
                time_major = True
            x, hr, hi = _ssm_mixer(x, w['norm_mix'], w['b_wb'], w['b_ab_re'], w['b_ab_im'],
                                   w['b_wc_re'], w['b_wc_im'], w['b_d'], w['b_w_glu'],
                                   h0_re[j], h0_im[j], i, j, n_seq, seq_len)
            new_re.append(hr)
            new_im.append(hi)
        if seq_len == 1:
            x = _attn_step(x, w['norm_x'], w['x_wq'], mem_k, mem_v, w['x_wo'], i)
        else:
            x = _attn_seq(x, w['norm_x'], w['x_wq'], mem_k, mem_v, w['x_wo'], i, n_seq, seq_len, time_major)
            if time_major:
                x = x.reshape(seq_len, n_seq, D_MODEL).transpose(1, 0, 2).reshape(-1, D_MODEL)
        x = _ffn(x, w['norm_ffn2'], w['ffn2_wg'], w['ffn2_wu'], w['ffn2_wd'], w['norm_final'], i,
                 i == DEPTH - 1)
    return x, new_re, new_im, new_v


def kernel(x_prompt, x_sample, cache_mem_k, cache_mem_v, state_ssm_re, state_ssm_im, mem_prompt,
           norm_ffn1, ffn1_wg, ffn1_wu, ffn1_wd, norm_mix,
           a_w_in, a_ln_g, a_ln_b, a_w_s, a_b_s, a_w_out,
           b_a_re, b_a_im, b_log_dt, b_b_re, b_b_im, b_c_re, b_c_im, b_d, b_w_glu,
           norm_x, norm_mem, x_wq, x_wk, x_wv, x_wo,
           norm_ffn2, ffn2_wg, ffn2_wu, ffn2_wd, norm_final):
    batch, seq, _ = x_prompt.shape
    dec_batch, dec_seq, _ = x_sample.shape
    assert dec_seq == 1 and seq % ROW_TILE == 0 and ROW_TILE % CHUNK == 0
    n_a = a_w_in.shape[0]
    n_b = b_a_re.shape[0]

    def vec(p):
        return p.reshape(p.shape[0], 1, p.shape[-1])

    ab_re, ab_im, bb_re, bb_im = _ssm_discretize(
        b_a_re, b_a_im, b_log_dt, b_b_re.transpose(0, 1, 3, 2), b_b_im.transpose(0, 1, 3, 2))
    wb = jnp.concatenate([_block_diag(bb_re, GROUP, STATE), _block_diag(bb_im, GROUP, STATE)],
                         axis=-1).astype(BF16)
    w = dict(
        norm_ffn1=vec(norm_ffn1), norm_ffn2=vec(norm_ffn2), norm_mix=vec(norm_mix), norm_x=vec(norm_x),
        norm_final=norm_final.reshape(1, D_MODEL),
        ffn1_wg=ffn1_wg.astype(BF16), ffn1_wu=ffn1_wu.astype(BF16), ffn1_wd=ffn1_wd.astype(BF16),
        ffn2_wg=ffn2_wg.astype(BF16), ffn2_wu=ffn2_wu.astype(BF16), ffn2_wd=ffn2_wd.astype(BF16),
        a_w_in=a_w_in.astype(BF16), a_w_out=a_w_out.astype(BF16),
        a_ln_g=vec(a_ln_g), a_ln_b=vec(a_ln_b), a_w_s=a_w_s,
        a_bias_full=jnp.repeat(jnp.swapaxes(a_b_s, 1, 2), A_HEAD_DIM, axis=-1),
        a_scale0=jnp.repeat(a_w_s[:, :, 0, 0], A_HEAD_DIM, axis=-1).reshape(n_a, 1, D_A),
        a_bias0=jnp.repeat(a_b_s[:, :, 0], A_HEAD_DIM, axis=-1).reshape(n_a, 1, D_A),
        b_wb=wb,
        b_ab_re=ab_re.reshape(n_b, 1, SSM_LANES), b_ab_im=ab_im.reshape(n_b, 1, SSM_LANES),
        b_wc_re=_block_diag(b_c_re.transpose(0, 1, 3, 2), STATE, GROUP).astype(BF16),
        b_wc_im=_block_diag(b_c_im.transpose(0, 1, 3, 2), STATE, GROUP).astype(BF16),
        b_d=b_d.reshape(n_b, 1, D_MODEL), b_w_glu=b_w_glu.astype(BF16),
        x_wq=x_wq.astype(BF16), x_wo=x_wo.astype(BF16),
    )

    mem_k, mem_v = _mem_kv(mem_prompt.reshape(batch * N_MEM, D_MODEL), vec(norm_mem),
                           x_wk.astype(BF16), x_wv.astype(BF16))
    mem_k = mem_k.reshape(DEPTH, batch, N_MEM, D_MODEL)
    mem_v = mem_v.reshape(DEPTH, batch, N_MEM, D_MODEL)
    h0 = jnp.zeros((n_b, batch, SSM_LANES), F32)
    y_prompt, p_re, p_im, _ = _trunk(x_prompt.reshape(batch * seq, D_MODEL), batch, seq,
                                     mem_k, mem_v, h0, h0, w)

    y_sample, s_re, s_im, s_v = _trunk(
        x_sample.reshape(dec_batch, D_MODEL), dec_batch, 1,
        cache_mem_k.reshape(DEPTH, dec_batch, N_MEM, D_MODEL),
        cache_mem_v.reshape(DEPTH, dec_batch, N_MEM, D_MODEL),
        state_ssm_re.reshape(n_b, dec_batch, SSM_LANES),
        state_ssm_im.reshape(n_b, dec_batch, SSM_LANES), w)

    def states(parts, n):
        return jnp.stack(parts).reshape(n_b, n, N_GROUPS, STATE)

    kv_shape = (DEPTH, batch, N_MEM, X_HEADS, X_HEAD_DIM)
    return (y_prompt.reshape(batch, seq, D_MODEL),
            y_sample.reshape(dec_batch, 1, D_MODEL),
            mem_k.reshape(kv_shape), mem_v.reshape(kv_shape),
            states(p_re, batch), states(p_im, batch),
            states(s_re, dec_batch), states(s_im, dec_batch),
            jnp.stack(s_v).reshape(n_a, dec_batch, 1, D_A))
```

```python
import functools
import math

import jax
import jax.numpy as jnp
from jax import lax
from jax.experimental import pallas as pl
from jax.experimental.pallas import tpu as pltpu

F32 = jnp.float32
BF16 = jnp.bfloat16

D_MODEL = 1024
DEPTH = 4
CHUNK = 128
D_A = 2 * D_MODEL
A_HEADS = 8
A_HEAD_DIM = D_A // A_HEADS
GROUP = 16
N_GROUPS = D_MODEL // GROUP
STATE = 64
SSM_LANES = N_GROUPS * STATE
N_MEM = 256
X_HEADS = 4
X_HEAD_DIM = D_MODEL // X_HEADS
D_FF = 2816
EPS = 1e-6

V7X_SUBLANES = 8
V7X_LANES = 128
V7X_VMEM_LIMIT_BYTES = 56 * 1024 * 1024

ROW_TILE = 512
SSM_TIME_TILE = 64
SSM_GROUP_BLOCK = 16
SSM_BLOCKS = N_GROUPS // SSM_GROUP_BLOCK
SCAN_LANES = 512
SAMPLE_ATTN_BATCH = 8


def _params(*semantics):
    return pltpu.CompilerParams(dimension_semantics=semantics,
                                vmem_limit_bytes=V7X_VMEM_LIMIT_BYTES)


def _const_spec(shape):
    nd = len(shape)
    return pl.BlockSpec(shape, lambda *_: (0,) * nd, pipeline_mode=pl.Buffered(1))


def _layer_spec(shape, layer):
    nd = len(shape)
    return pl.BlockSpec((None,) + tuple(shape), lambda *_: (layer,) + (0,) * nd,
                        pipeline_mode=pl.Buffered(1))


def _rmsnorm(x, g):
    return x * lax.rsqrt(jnp.mean(x * x, axis=-1, keepdims=True) + EPS) * g


def _dot(a, b):
    return jnp.dot(a, b, preferred_element_type=F32)


def _ffn_kernel(x_ref, g_ref, wg_ref, wu_ref, wd_ref, gf_ref, o_ref, *, final_norm):
    x = x_ref[...]
    h = _rmsnorm(x, g_ref[...]).astype(BF16)
    gate = _dot(h, wg_ref[...])
    up = _dot(h, wu_ref[...])
    act = (jax.nn.silu(gate) * up).astype(BF16)
    y = x + 0.5 * _dot(act, wd_ref[...])
    if final_norm:
        y = _rmsnorm(y, gf_ref[...])
    o_ref[...] = y


def _ffn(x, g, wg, wu, wd, g_final, layer, final_norm):
    rows = x.shape[0]
    tm = min(ROW_TILE, rows)
    row_spec = pl.BlockSpec((tm, D_MODEL), lambda i: (i, 0))
    return pl.pallas_call(
        functools.partial(_ffn_kernel, final_norm=final_norm),
        out_shape=jax.ShapeDtypeStruct(x.shape, F32),
        grid=(rows // tm,),
        in_specs=[row_spec,
                  _layer_spec((1, D_MODEL), layer),
                  _layer_spec((D_MODEL, D_FF), layer),
                  _layer_spec((D_MODEL, D_FF), layer),
                  _layer_spec((D_FF, D_MODEL), layer),
                  _const_spec((1, D_MODEL))],
        out_specs=row_spec,
        compiler_params=_params("parallel"),
        name="ffn",
    )(x, g, wg, wu, wd, g_final)


def _gated_unit_inputs(x_ref, g_ref, w_in_ref, ln_g_ref, ln_b_ref):
    h = _rmsnorm(x_ref[...], g_ref[...]).astype(BF16)
    uv = jax.nn.gelu(_dot(h, w_in_ref[...]))
    u = uv[:, :D_A]
    v = uv[:, D_A:]
    mu = jnp.mean(v, axis=-1, keepdims=True)
    vc = v - mu
    var = jnp.mean(vc * vc, axis=-1, keepdims=True)
    v = vc * lax.rsqrt(var + EPS) * ln_g_ref[...] + ln_b_ref[...]
    return u, v


def _mixer_a_seq_kernel(x_ref, g_ref, w_in_ref, ln_g_ref, ln_b_ref, ws_ref, bias_ref, w_out_ref,
                        o_ref, gated_ref):
    u, v = _gated_unit_inputs(x_ref, g_ref, w_in_ref, ln_g_ref, ln_b_ref)
    vb = v.astype(BF16)
    t_idx = lax.broadcasted_iota(jnp.int32, (CHUNK, CHUNK), 0)
    s_idx = lax.broadcasted_iota(jnp.int32, (CHUNK, CHUNK), 1)
    causal = t_idx >= s_idx
    for head in range(A_HEADS):
        cols = slice(head * A_HEAD_DIM, (head + 1) * A_HEAD_DIM)
        ws = jnp.where(causal, ws_ref[head], 0.0).astype(BF16)
        bias = bias_ref[:, cols]
        for c in range(x_ref.shape[0] // CHUNK):
            rows = slice(c * CHUNK, (c + 1) * CHUNK)
            mixed = _dot(ws, vb[rows, cols]) + bias
            gated_ref[rows, cols] = (u[rows, cols] * mixed).astype(BF16)
    o_ref[...] = x_ref[...] + _dot(gated_ref[...], w_out_ref[...])


def _mixer_a_step_kernel(x_ref, g_ref, w_in_ref, ln_g_ref, ln_b_ref, scale_ref, bias_ref, w_out_ref,
                         o_ref, v_ref):
    u, v = _gated_unit_inputs(x_ref, g_ref, w_in_ref, ln_g_ref, ln_b_ref)
    v_ref[...] = v
    mixed = v * scale_ref[...] + bias_ref[...]
    o_ref[...] = x_ref[...] + _dot((u * mixed).astype(BF16), w_out_ref[...])


def _mixer_a_seq(x, g, w_in, ln_g, ln_b, w_s, bias_full, w_out, layer, j):
    rows = x.shape[0]
    tm = ROW_TILE
    row_spec = pl.BlockSpec((tm, D_MODEL), lambda i: (i, 0))
    return pl.pallas_call(
        _mixer_a_seq_kernel,
        out_shape=jax.ShapeDtypeStruct(x.shape, F32),
        grid=(rows // tm,),
        in_specs=[row_spec,
                  _layer_spec((1, D_MODEL), layer),
                  _layer_spec((D_MODEL, 2 * D_A), j),
                  _layer_spec((1, D_A), j),
                  _layer_spec((1, D_A), j),
                  _layer_spec((A_HEADS, CHUNK, CHUNK), j),
                  _layer_spec((CHUNK, D_A), j),
                  _layer_spec((D_A, D_MODEL), j)],
        out_specs=row_spec,
        scratch_shapes=[pltpu.VMEM((tm, D_A), BF16)],
        compiler_params=_params("parallel"),
        name="mixer_a_seq",
    )(x, g, w_in, ln_g, ln_b, w_s, bias_full, w_out)


def _mixer_a_step(x, g, w_in, ln_g, ln_b, scale, bias, w_out, layer, j):
    rows = x.shape[0]
    return pl.pallas_call(
        _mixer_a_step_kernel,
        out_shape=(jax.ShapeDtypeStruct(x.shape, F32), jax.ShapeDtypeStruct((rows, D_A), F32)),
        grid=(1,),
        in_specs=[_const_spec((rows, D_MODEL)),
                  _layer_spec((1, D_MODEL), layer),
                  _layer_spec((D_MODEL, 2 * D_A), j),
                  _layer_spec((1, D_A), j),
                  _layer_spec((1, D_A), j),
                  _layer_spec((1, D_A), j),
                  _layer_spec((1, D_A), j),
                  _layer_spec((D_A, D_MODEL), j)],
        out_specs=(_const_spec((rows, D_MODEL)), _const_spec((rows, D_A))),
        compiler_params=_params("arbitrary"),
        name="mixer_a_step",
    )(x, g, w_in, ln_g, ln_b, scale, bias, w_out)


def _ssm_discretize_kernel(a_re_ref, a_im_ref, log_dt_ref, b_re_ref, b_im_ref,
                           ab_re_ref, ab_im_ref, bb_re_ref, bb_im_ref):
    a_re = a_re_ref[...]
    a_im = a_im_ref[...]
    dt = jnp.exp(log_dt_ref[...])
    mag = jnp.exp(a_re * dt)
    ab_re = mag * jnp.cos(a_im * dt)
    ab_im = mag * jnp.sin(a_im * dt)
    den = a_re * a_re + a_im * a_im
    nr = ab_re - 1.0
    ni = ab_im
    f_re = (nr * a_re + ni * a_im) / den
    f_im = (ni * a_re - nr * a_im) / den
    b_re = b_re_ref[...]
    b_im = b_im_ref[...]
    ab_re_ref[...] = ab_re
    ab_im_ref[...] = ab_im
    bb_re_ref[...] = f_re * b_re - f_im * b_im
    bb_im_ref[...] = f_re * b_im + f_im * b_re


def _ssm_discretize(a_re, a_im, log_dt, b_re_t, b_im_t):
    n_layers = a_re.shape[0]
    ab_shape = jax.ShapeDtypeStruct((n_layers, N_GROUPS, 1, STATE), F32)
    bb_shape = jax.ShapeDtypeStruct((n_layers, N_GROUPS, GROUP, STATE), F32)
    return pl.pallas_call(
        _ssm_discretize_kernel,
        out_shape=(ab_shape, ab_shape, bb_shape, bb_shape),
        name="ssm_discretize",
    )(a_re.reshape(ab_shape.shape), a_im.reshape(ab_shape.shape),
      log_dt.reshape(n_layers, N_GROUPS, 1, 1), b_re_t, b_im_t)


def _ssm_kernel(x_ref, g_ref, wb_ref, ab_re_ref, ab_im_ref, wc_re_ref, wc_im_ref, d_ref, wglu_ref,
                h0_re_ref, h0_im_ref, o_ref, hf_re_ref, hf_im_ref,
                s_re_ref, s_im_ref, st_re_ref, st_im_ref, tm_ref, *, n_batch, n_time):
    step = pl.program_id(0)
    blk = SSM_GROUP_BLOCK * STATE
    cblk = SSM_GROUP_BLOCK * GROUP
    lane_tiles = [slice(j * V7X_LANES, (j + 1) * V7X_LANES) for j in range(D_MODEL // V7X_LANES)]

    @pl.when(step == 0)
    def _():
        st_re_ref[...] = h0_re_ref[...]
        st_im_ref[...] = h0_im_ref[...]

    if n_time == 1:
        u = _rmsnorm(x_ref[...], g_ref[...])
    else:
        for b in range(n_batch):
            ub_rows = _rmsnorm(x_ref[b], g_ref[...])
            for j, cols in enumerate(lane_tiles):
                tm_ref[j, pl.ds(b, n_time, stride=n_batch), :] = ub_rows[:, cols]
        u = jnp.concatenate([tm_ref[j] for j in range(len(lane_tiles))], axis=1)
    ub = u.astype(BF16)
    for k in range(SSM_BLOCKS):
        bu = _dot(ub[:, k * cblk:(k + 1) * cblk], wb_ref[k])
        s_re_ref[:, k * blk:(k + 1) * blk] = bu[:, :blk]
        s_im_ref[:, k * blk:(k + 1) * blk] = bu[:, blk:]

    if n_time == 1:
        ar = ab_re_ref[...]
        ai = ab_im_ref[...]
        hr0 = st_re_ref[...]
        hi0 = st_im_ref[...]
        hr = ar * hr0 - ai * hi0 + s_re_ref[...]
        hi = ar * hi0 + ai * hr0 + s_im_ref[...]
        s_re_ref[...] = hr
        s_im_ref[...] = hi
        st_re_ref[...] = hr
        st_im_ref[...] = hi
    else:
        for c in range(SSM_LANES // SCAN_LANES):
            lanes = slice(c * SCAN_LANES, (c + 1) * SCAN_LANES)
            ar = jnp.broadcast_to(ab_re_ref[:, lanes], (V7X_SUBLANES, SCAN_LANES))
            ai = jnp.broadcast_to(ab_im_ref[:, lanes], (V7X_SUBLANES, SCAN_LANES))
            for s in range(n_batch // V7X_SUBLANES):
                rows0 = slice(s * V7X_SUBLANES, (s + 1) * V7X_SUBLANES)

                def body(t, carry, s=s, lanes=lanes, ar=ar, ai=ai):
                    hr, hi = carry
                    row = pl.multiple_of(t * n_batch + s * V7X_SUBLANES, V7X_SUBLANES)
                    rows = pl.ds(row, V7X_SUBLANES)
                    nr = ar * hr - ai * hi + s_re_ref[rows, lanes]
                    ni = ar * hi + ai * hr + s_im_ref[rows, lanes]
                    s_re_ref[rows, lanes] = nr
                    s_im_ref[rows, lanes] = ni
                    return nr, ni

                hr, hi = lax.fori_loop(0, n_time, body,
                                       (st_re_ref[rows0, lanes], st_im_ref[rows0, lanes]), unroll=2)
                st_re_ref[rows0, lanes] = hr
                st_im_ref[rows0, lanes] = hi

    ys = []
    for k in range(SSM_BLOCKS):
        hr = s_re_ref[:, k * blk:(k + 1) * blk].astype(BF16)
        hi = s_im_ref[:, k * blk:(k + 1) * blk].astype(BF16)
        ys.append(_dot(hr, wc_re_ref[k]) - _dot(hi, wc_im_ref[k]))
    y = jnp.concatenate(ys, axis=-1) + d_ref[...] * u
    z = jax.nn.gelu(y).astype(BF16)
    vg = _dot(z, wglu_ref[...])
    out = vg[:, :D_MODEL] * jax.nn.sigmoid(vg[:, D_MODEL:])
    if n_time == 1:
        o_ref[...] = x_ref[...] + out
    else:
        for j, cols in enumerate(lane_tiles):
            tm_ref[j] = out[:, cols]
        for b in range(n_batch):
            rows = pl.ds(b, n_time, stride=n_batch)
            out_b = jnp.concatenate([tm_ref[j, rows, :] for j in range(len(lane_tiles))], axis=1)
            o_ref[b] = x_ref[b] + out_b

    @pl.when(step == pl.num_programs(0) - 1)
    def _():
        hf_re_ref[...] = st_re_ref[...]
        hf_im_ref[...] = st_im_ref[...]


def _ssm_mixer(x, g, wb, ab_re, ab_im, wc_re, wc_im, d_skip, w_glu, h0_re, h0_im,
               layer, j, n_batch, n_time_total):
    n_time = min(SSM_TIME_TILE, n_time_total)
    tile = n_time * n_batch
    if n_time_total == 1:
        x_in = x
        row_spec = _const_spec((n_batch, D_MODEL))
    else:
        x_in = x.reshape(n_batch, n_time_total, D_MODEL)
        row_spec = pl.BlockSpec((n_batch, n_time, D_MODEL), lambda i: (0, i, 0))
    state_spec = _const_spec((n_batch, SSM_LANES))
    state_shape = jax.ShapeDtypeStruct((n_batch, SSM_LANES), F32)
    blk = SSM_GROUP_BLOCK * STATE
    cblk = SSM_GROUP_BLOCK * GROUP
    out, h_re, h_im = pl.pallas_call(
        functools.partial(_ssm_kernel, n_batch=n_batch, n_time=n_time),
        out_shape=(jax.ShapeDtypeStruct(x_in.shape, F32), state_shape, state_shape),
        grid=(n_time_total // n_time,),
        in_specs=[row_spec,
                  _layer_spec((1, D_MODEL), layer),
                  _layer_spec((SSM_BLOCKS, cblk, 2 * blk), j),
                  _layer_spec((1, SSM_LANES), j),
                  _layer_spec((1, SSM_LANES), j),
                  _layer_spec((SSM_BLOCKS, blk, cblk), j),
                  _layer_spec((SSM_BLOCKS, blk, cblk), j),
                  _layer_spec((1, D_MODEL), j),
                  _layer_spec((D_MODEL, 2 * D_MODEL), j),
                  state_spec, state_spec],
        out_specs=(row_spec, state_spec, state_spec),
        scratch_shapes=[pltpu.VMEM((tile, SSM_LANES), F32), pltpu.VMEM((tile, SSM_LANES), F32),
                        pltpu.VMEM((n_batch, SSM_LANES), F32), pltpu.VMEM((n_batch, SSM_LANES), F32),
                        pltpu.VMEM((D_MODEL // V7X_LANES, tile, V7X_LANES), F32)],
        compiler_params=_params("arbitrary"),
        name="ssm_mixer",
    )(x_in, g, wb, ab_re, ab_im, wc_re, wc_im, d_skip, w_glu, h0_re, h0_im)
    return out.reshape(x.shape), h_re, h_im


def _block_diag(w, rows_inner, cols_inner):
    n_layers = w.shape[0]
    gb = SSM_GROUP_BLOCK
    w = w.reshape(n_layers, SSM_BLOCKS, gb, rows_inner, cols_inner)
    eye = jnp.eye(gb, dtype=w.dtype)
    out = jnp.einsum('lkgrc,gh->lkgrhc', w, eye)
    return out.reshape(n_layers, SSM_BLOCKS, gb * rows_inner, gb * cols_inner)


def _mem_kv_kernel(m_ref, g_ref, wk_ref, wv_ref, k_ref, v_ref):
    m = _rmsnorm(m_ref[...], g_ref[...]).astype(BF16)
    k_ref[...] = _dot(m, wk_ref[...])
    v_ref[...] = _dot(m, wv_ref[...])


def _mem_kv(mem, g, wk, wv):
    rows = mem.shape[0]
    tm = ROW_TILE
    out = jax.ShapeDtypeStruct((DEPTH, rows, D_MODEL), F32)
    w_spec = pl.BlockSpec((None, D_MODEL, D_MODEL), lambda l, i: (l, 0, 0))
    o_spec = pl.BlockSpec((None, tm, D_MODEL), lambda l, i: (l, i, 0))
    return pl.pallas_call(
        _mem_kv_kernel,
        out_shape=(out, out),
        grid=(DEPTH, rows // tm),
        in_specs=[pl.BlockSpec((tm, D_MODEL), lambda l, i: (i, 0)),
                  pl.BlockSpec((None, 1, D_MODEL), lambda l, i: (l, 0, 0)),
                  w_spec, w_spec],
        out_specs=(o_spec, o_spec),
        compiler_params=_params("parallel", "parallel"),
        name="mem_kv",
    )(mem, g, wk, wv)


def _softmax_rows(s):
    e = jnp.exp(s - jnp.max(s, axis=-1, keepdims=True))
    return e / jnp.sum(e, axis=-1, keepdims=True)


def _attn_seq_kernel(x_ref, g_ref, wq_ref, k_ref, v_ref, wo_ref, o_ref, oh_ref):
    x = x_ref[...]
    h = _rmsnorm(x, g_ref[...]).astype(BF16)
    q = _dot(h, wq_ref[...])
    kb = k_ref[...].astype(BF16)
    vb = v_ref[...].astype(BF16)
    scale = X_HEAD_DIM ** -0.5
    for head in range(X_HEADS):
        cols = slice(head * X_HEAD_DIM, (head + 1) * X_HEAD_DIM)
        s = lax.dot_general(q[:, cols].astype(BF16), kb[:, cols], (((1,), (1,)), ((), ())),
                            preferred_element_type=F32) * scale
        p = _softmax_rows(s).astype(BF16)
        oh_ref[:, cols] = _dot(p, vb[:, cols]).astype(BF16)
    o_ref[...] = x + _dot(oh_ref[...], wo_ref[...])


def _attn_seq(x, g, wq, k, v, wo, layer, n_seq, seq_len):
    tm = ROW_TILE
    n_t = seq_len // tm
    row_spec = pl.BlockSpec((tm, D_MODEL), lambda b, i: (b * n_t + i, 0))
    kv_spec = pl.BlockSpec((None, None, N_MEM, D_MODEL), lambda b, i: (layer, b, 0, 0))
    return pl.pallas_call(
        _attn_seq_kernel,
        out_shape=jax.ShapeDtypeStruct(x.shape, F32),
        grid=(n_seq, n_t),
        in_specs=[row_spec,
                  _layer_spec((1, D_MODEL), layer),
                  _layer_spec((D_MODEL, D_MODEL), layer),
                  kv_spec, kv_spec,
                  _layer_spec((D_MODEL, D_MODEL), layer)],
        out_specs=row_spec,
        scratch_shapes=[pltpu.VMEM((tm, D_MODEL), BF16)],
        compiler_params=_params("parallel", "parallel"),
        name="attn_seq",
    )(x, g, wq, k, v, wo)


def _attn_step_kernel(x_ref, g_ref, wq_ref, k_ref, v_ref, wo_ref, o_ref, q_ref, oh_ref):
    step = pl.program_id(0)
    nb = SAMPLE_ATTN_BATCH

    @pl.when(step == 0)
    def _():
        h = _rmsnorm(x_ref[...], g_ref[...]).astype(BF16)
        q_ref[...] = _dot(h, wq_ref[...])

    row0 = pl.multiple_of(step * nb, nb)
    q = q_ref[pl.ds(row0, nb), :]
    n_rows = N_MEM * X_HEADS
    head_of_lane = lax.broadcasted_iota(jnp.int32, (V7X_SUBLANES, n_rows), 1) % X_HEADS
    head_of_row = lax.broadcasted_iota(jnp.int32, (V7X_SUBLANES, n_rows), 0) % X_HEADS
    own_head = head_of_lane == head_of_row
    scale = X_HEAD_DIM ** -0.5
    head_cols = [slice(h * X_HEAD_DIM, (h + 1) * X_HEAD_DIM) for h in range(X_HEADS)]
    outs = []
    for s in range(nb):
        qs = q[s:s + 1, :]
        qh = jnp.concatenate([qs[:, c] for c in head_cols] * (V7X_SUBLANES // X_HEADS), axis=0)
        k2 = k_ref[s].reshape(n_rows, X_HEAD_DIM).astype(BF16)
        v2 = v_ref[s].reshape(n_rows, X_HEAD_DIM).astype(BF16)
        sc = lax.dot_general(qh.astype(BF16), k2, (((1,), (1,)), ((), ())),
                             preferred_element_type=F32) * scale
        p = _softmax_rows(jnp.where(own_head, sc, -jnp.inf)).astype(BF16)
        pv = _dot(p, v2)
        outs.append(jnp.concatenate([pv[h:h + 1, :] for h in range(X_HEADS)], axis=1))
    oh_ref[pl.ds(row0, nb), :] = jnp.concatenate(outs, axis=0)

    @pl.when(step == pl.num_programs(0) - 1)
    def _():
        o_ref[...] = x_ref[...] + _dot(oh_ref[...].astype(BF16), wo_ref[...])


def _attn_step(x, g, wq, k, v, wo, layer):
    rows = x.shape[0]
    nb = SAMPLE_ATTN_BATCH
    kv_spec = pl.BlockSpec((None, nb, N_MEM, X_HEADS, X_HEAD_DIM), lambda i: (layer, i, 0, 0, 0))
    return pl.pallas_call(
        _attn_step_kernel,
        out_shape=jax.ShapeDtypeStruct(x.shape, F32),
        grid=(rows // nb,),
        in_specs=[_const_spec((rows, D_MODEL)),
                  _layer_spec((1, D_MODEL), layer),
                  _layer_spec((D_MODEL, D_MODEL), layer),
                  kv_spec, kv_spec,
                  _layer_spec((D_MODEL, D_MODEL), layer)],
        out_specs=_const_spec((rows, D_MODEL)),
        scratch_shapes=[pltpu.VMEM((rows, D_MODEL), F32), pltpu.VMEM((rows, D_MODEL), F32)],
        compiler_params=_params("arbitrary"),
        name="attn_step",
    )(x, g, wq, k, v, wo)


def _trunk(x, n_seq, seq_len, mem_k, mem_v, h0_re, h0_im, w):
    new_re, new_im, new_v = [], [], []
    for i in range(DEPTH):
        j = i // 2
        x = _ffn(x, w['norm_ffn1'], w['ffn1_wg'], w['ffn1_wu'], w['ffn1_wd'], w['norm_final'], i, False)
        if i % 2 == 0:
            if seq_len == 1:
                x, v_rows = _mixer_a_step(x, w['norm_mix'], w['a_w_in'], w['a_ln_g'], w['a_ln_b'],
                                          w['a_scale0'], w['a_bias0'], w['a_w_out'], i, j)
                new_v.append(v_rows)
            else:
                x = _mixer_a_seq(x, w['norm_mix'], w['a_w_in'], w['a_ln_g'], w['a_ln_b'],
                                 w['a_w_s'], w['a_bias_full'], w['a_w_out'], i, j)
        else:
            x, hr, hi = _ssm_mixer(x, w['norm_mix'], w['b_wb'], w['b_ab_re'], w['b_ab_im'],
                                   w['b_wc_re'], w['b_wc_im'], w['b_d'], w['b_w_glu'],
                                   h0_re[j], h0_im[j], i, j, n_seq, seq_len)
            new_re.append(hr)
            new_im.append(hi)
        if seq_len == 1:
            x = _attn_step(x, w['norm_x'], w['x_wq'], mem_k, mem_v, w['x_wo'], i)
        else:
            x = _attn_seq(x, w['norm_x'], w['x_wq'], mem_k, mem_v, w['x_wo'], i, n_seq, seq_len)
        x = _ffn(x, w['norm_ffn2'], w['ffn2_wg'], w['ffn2_wu'], w['ffn2_wd'], w['norm_final'], i,
                 i == DEPTH - 1)
    return x, new_re, new_im, new_v


def kernel(x_prompt, x_sample, cache_mem_k, cache_mem_v, state_ssm_re, state_ssm_im, mem_prompt,
           norm_ffn1, ffn1_wg, ffn1_wu, ffn1_wd, norm_mix,
           a_w_in, a_ln_g, a_ln_b, a_w_s, a_b_s, a_w_out,
           b_a_re, b_a_im, b_log_dt, b_b_re, b_b_im, b_c_re, b_c_im, b_d, b_w_glu,
           norm_x, norm_mem, x_wq, x_wk, x_wv, x_wo,
           norm_ffn2, ffn2_wg, ffn2_wu, ffn2_wd, norm_final):
    batch, seq, _ = x_prompt.shape
    dec_batch, dec_seq, _ = x_sample.shape
    assert dec_seq == 1 and seq % ROW_TILE == 0 and ROW_TILE % CHUNK == 0
    n_a = a_w_in.shape[0]
    n_b = b_a_re.shape[0]

    def vec(p):
        return p.reshape(p.shape[0], 1, p.shape[-1])

    ab_re, ab_im, bb_re, bb_im = _ssm_discretize(
        b_a_re, b_a_im, b_log_dt, b_b_re.transpose(0, 1, 3, 2), b_b_im.transpose(0, 1, 3, 2))
    wb = jnp.concatenate([_block_diag(bb_re, GROUP, STATE), _block_diag(bb_im, GROUP, STATE)],
                         axis=-1).astype(BF16)
    w = dict(
        norm_ffn1=vec(norm_ffn1), norm_ffn2=vec(norm_ffn2), norm_mix=vec(norm_mix), norm_x=vec(norm_x),
        norm_final=norm_final.reshape(1, D_MODEL),
        ffn1_wg=ffn1_wg.astype(BF16), ffn1_wu=ffn1_wu.astype(BF16), ffn1_wd=ffn1_wd.astype(BF16),
        ffn2_wg=ffn2_wg.astype(BF16), ffn2_wu=ffn2_wu.astype(BF16), ffn2_wd=ffn2_wd.astype(BF16),
        a_w_in=a_w_in.astype(BF16), a_w_out=a_w_out.astype(BF16),
        a_ln_g=vec(a_ln_g), a_ln_b=vec(a_ln_b), a_w_s=a_w_s,
        a_bias_full=jnp.repeat(jnp.swapaxes(a_b_s, 1, 2), A_HEAD_DIM, axis=-1),
        a_scale0=jnp.repeat(a_w_s[:, :, 0, 0], A_HEAD_DIM, axis=-1).reshape(n_a, 1, D_A),
        a_bias0=jnp.repeat(a_b_s[:, :, 0], A_HEAD_DIM, axis=-1).reshape(n_a, 1, D_A),
        b_wb=wb,
        b_ab_re=ab_re.reshape(n_b, 1, SSM_LANES), b_ab_im=ab_im.reshape(n_b, 1, SSM_LANES),
        b_wc_re=_block_diag(b_c_re.transpose(0, 1, 3, 2), STATE, GROUP).astype(BF16),
        b_wc_im=_block_diag(b_c_im.transpose(0, 1, 3, 2), STATE, GROUP).astype(BF16),
        b_d=b_d.reshape(n_b, 1, D_MODEL), b_w_glu=b_w_glu.astype(BF16),
        x_wq=x_wq.astype(BF16), x_wo=x_wo.astype(BF16),
    )

    mem_k, mem_v = _mem_kv(mem_prompt.reshape(batch * N_MEM, D_MODEL), vec(norm_mem),
                           x_wk.astype(BF16), x_wv.astype(BF16))
    mem_k = mem_k.reshape(DEPTH, batch, N_MEM, D_MODEL)
    mem_v = mem_v.reshape(DEPTH, batch, N_MEM, D_MODEL)
    h0 = jnp.zeros((n_b, batch, SSM_LANES), F32)
    y_prompt, p_re, p_im, _ = _trunk(x_prompt.reshape(batch * seq, D_MODEL), batch, seq,
                                     mem_k, mem_v, h0, h0, w)

    y_sample, s_re, s_im, s_v = _trunk(
        x_sample.reshape(dec_batch, D_MODEL), dec_batch, 1, cache_mem_k, cache_mem_v,
        state_ssm_re.reshape(n_b, dec_batch, SSM_LANES),
        state_ssm_im.reshape(n_b, dec_batch, SSM_LANES), w)

    def states(parts, n):
        return jnp.stack(parts).reshape(n_b, n, N_GROUPS, STATE)

    kv_shape = (DEPTH, batch, N_MEM, X_HEADS, X_HEAD_DIM)
    return (y_prompt.reshape(batch, seq, D_MODEL),
            y_sample.reshape(dec_batch, 1, D_MODEL),
            mem_k.reshape(kv_shape), mem_v.reshape(kv_shape),
            states(p_re, batch), states(p_im, batch),
            states(s_re, dec_batch), states(s_im, dec_batch),
            jnp.stack(s_v).reshape(n_a, dec_batch, 1, D_A))
```

```python
import functools
import math

import jax
import jax.numpy as jnp
from jax import lax
from jax.experimental import pallas as pl
from jax.experimental.pallas import tpu as pltpu

F32 = jnp.float32
BF16 = jnp.bfloat16

D_MODEL = 1024
DEPTH = 4
CHUNK = 128
D_A = 2 * D_MODEL
A_HEADS = 8
A_HEAD_DIM = D_A // A_HEADS
GROUP = 16
N_GROUPS = D_MODEL // GROUP
STATE = 64
SSM_LANES = N_GROUPS * STATE
N_MEM = 256
X_HEADS = 4
X_HEAD_DIM = D_MODEL // X_HEADS
D_FF = 2816
EPS = 1e-6

V7X_SUBLANES = 8
V7X_LANES = 128
V7X_VMEM_LIMIT_BYTES = 56 * 1024 * 1024

ROW_TILE = 512
SSM_TIME_TILE = 64
SSM_GROUP_BLOCK = 16
SSM_BLOCKS = N_GROUPS // SSM_GROUP_BLOCK
SCAN_LANES = 512
SAMPLE_ATTN_BATCH = 8


def _params(*semantics):
    return pltpu.CompilerParams(dimension_semantics=semantics,
                                vmem_limit_bytes=V7X_VMEM_LIMIT_BYTES)


def _const_spec(shape):
    nd = len(shape)
    return pl.BlockSpec(shape, lambda *_: (0,) * nd, pipeline_mode=pl.Buffered(1))


def _layer_spec(shape, layer):
    nd = len(shape)
    return pl.BlockSpec((None,) + tuple(shape), lambda *_: (layer,) + (0,) * nd,
                        pipeline_mode=pl.Buffered(1))


def _rmsnorm(x, g):
    return x * lax.rsqrt(jnp.mean(x * x, axis=-1, keepdims=True) + EPS) * g


def _dot(a, b):
    return jnp.dot(a, b, preferred_element_type=F32)


def _ffn_kernel(x_ref, xs_ref, g_ref, wg_ref, wu_ref, wd_ref, gf_ref, o_ref, os_ref, *, final_norm):
    def ffn_rows(x):
        h = _rmsnorm(x, g_ref[...]).astype(BF16)
        gate = _dot(h, wg_ref[...])
        up = _dot(h, wu_ref[...])
        act = (jax.nn.silu(gate) * up).astype(BF16)
        y = x + 0.5 * _dot(act, wd_ref[...])
        return _rmsnorm(y, gf_ref[...]) if final_norm else y

    o_ref[...] = ffn_rows(x_ref[...])

    @pl.when(pl.program_id(0) == pl.num_programs(0) - 1)
    def _():
        os_ref[...] = ffn_rows(xs_ref[...])


def _ffn(x, xs, g, wg, wu, wd, g_final, layer, final_norm):
    rows = x.shape[0]
    tm = ROW_TILE
    row_spec = pl.BlockSpec((tm, D_MODEL), lambda i: (i, 0))
    return pl.pallas_call(
        functools.partial(_ffn_kernel, final_norm=final_norm),
        out_shape=(jax.ShapeDtypeStruct(x.shape, F32), jax.ShapeDtypeStruct(xs.shape, F32)),
        grid=(rows // tm,),
        in_specs=[row_spec,
                  _const_spec(xs.shape),
                  _layer_spec((1, D_MODEL), layer),
                  _layer_spec((D_MODEL, D_FF), layer),
                  _layer_spec((D_MODEL, D_FF), layer),
                  _layer_spec((D_FF, D_MODEL), layer),
                  _const_spec((1, D_MODEL))],
        out_specs=(row_spec, _const_spec(xs.shape)),
        compiler_params=_params("arbitrary"),
        name="ffn",
    )(x, xs, g, wg, wu, wd, g_final)


def _gated_unit_inputs(x_ref, g_ref, w_in_ref, ln_g_ref, ln_b_ref):
    h = _rmsnorm(x_ref[...], g_ref[...]).astype(BF16)
    uv = jax.nn.gelu(_dot(h, w_in_ref[...]))
    u = uv[:, :D_A]
    v = uv[:, D_A:]
    mu = jnp.mean(v, axis=-1, keepdims=True)
    vc = v - mu
    var = jnp.mean(vc * vc, axis=-1, keepdims=True)
    v = vc * lax.rsqrt(var + EPS) * ln_g_ref[...] + ln_b_ref[...]
    return u, v


def _mixer_a_kernel(x_ref, xs_ref, g_ref, w_in_ref, ln_g_ref, ln_b_ref, ws_ref, bias_ref,
                    scale0_ref, bias0_ref, w_out_ref, o_ref, os_ref, vs_ref, gated_ref):
    u, v = _gated_unit_inputs(x_ref, g_ref, w_in_ref, ln_g_ref, ln_b_ref)
    vb = v.astype(BF16)
    t_idx = lax.broadcasted_iota(jnp.int32, (CHUNK, CHUNK), 0)
    s_idx = lax.broadcasted_iota(jnp.int32, (CHUNK, CHUNK), 1)
    causal = t_idx >= s_idx
    for head in range(A_HEADS):
        cols = slice(head * A_HEAD_DIM, (head + 1) * A_HEAD_DIM)
        ws = jnp.where(causal, ws_ref[head], 0.0).astype(BF16)
        bias = bias_ref[:, cols]
        for c in range(x_ref.shape[0] // CHUNK):
            rows = slice(c * CHUNK, (c + 1) * CHUNK)
            mixed = _dot(ws, vb[rows, cols]) + bias
            gated_ref[rows, cols] = (u[rows, cols] * mixed).astype(BF16)
    o_ref[...] = x_ref[...] + _dot(gated_ref[...], w_out_ref[...])

    @pl.when(pl.program_id(0) == pl.num_programs(0) - 1)
    def _():
        us, vs = _gated_unit_inputs(xs_ref, g_ref, w_in_ref, ln_g_ref, ln_b_ref)
        vs_ref[...] = vs
        mixed = vs * scale0_ref[...] + bias0_ref[...]
        os_ref[...] = xs_ref[...] + _dot((us * mixed).astype(BF16), w_out_ref[...])


def _mixer_a(x, xs, g, w_in, ln_g, ln_b, w_s, bias_full, scale0, bias0, w_out, layer, j):
    rows = x.shape[0]
    tm = ROW_TILE
    row_spec = pl.BlockSpec((tm, D_MODEL), lambda i: (i, 0))
    vs_shape = (xs.shape[0], D_A)
    return pl.pallas_call(
        _mixer_a_kernel,
        out_shape=(jax.ShapeDtypeStruct(x.shape, F32), jax.ShapeDtypeStruct(xs.shape, F32),
                   jax.ShapeDtypeStruct(vs_shape, F32)),
        grid=(rows // tm,),
        in_specs=[row_spec,
                  _const_spec(xs.shape),
                  _layer_spec((1, D_MODEL), layer),
                  _layer_spec((D_MODEL, 2 * D_A), j),
                  _layer_spec((1, D_A), j),
                  _layer_spec((1, D_A), j),
                  _layer_spec((A_HEADS, CHUNK, CHUNK), j),
                  _layer_spec((CHUNK, D_A), j),
                  _layer_spec((1, D_A), j),
                  _layer_spec((1, D_A), j),
                  _layer_spec((D_A, D_MODEL), j)],
        out_specs=(row_spec, _const_spec(xs.shape), _const_spec(vs_shape)),
        scratch_shapes=[pltpu.VMEM((tm, D_A), BF16)],
        compiler_params=_params("arbitrary"),
        name="mixer_a",
    )(x, xs, g, w_in, ln_g, ln_b, w_s, bias_full, scale0, bias0, w_out)


def _ssm_discretize_kernel(a_re_ref, a_im_ref, log_dt_ref, b_re_ref, b_im_ref,
                           ab_re_ref, ab_im_ref, bb_re_ref, bb_im_ref):
    a_re = a_re_ref[...]
    a_im = a_im_ref[...]
    dt = jnp.exp(log_dt_ref[...])
    mag = jnp.exp(a_re * dt)
    ab_re = mag * jnp.cos(a_im * dt)
    ab_im = mag * jnp.sin(a_im * dt)
    den = a_re * a_re + a_im * a_im
    nr = ab_re - 1.0
    ni = ab_im
    f_re = (nr * a_re + ni * a_im) / den
    f_im = (ni * a_re - nr * a_im) / den
    b_re = b_re_ref[...]
    b_im = b_im_ref[...]
    ab_re_ref[...] = ab_re
    ab_im_ref[...] = ab_im
    bb_re_ref[...] = f_re * b_re - f_im * b_im
    bb_im_ref[...] = f_re * b_im + f_im * b_re


def _ssm_discretize(a_re, a_im, log_dt, b_re_t, b_im_t):
    n_layers = a_re.shape[0]
    ab_shape = jax.ShapeDtypeStruct((n_layers, N_GROUPS, 1, STATE), F32)
    bb_shape = jax.ShapeDtypeStruct((n_layers, N_GROUPS, GROUP, STATE), F32)
    return pl.pallas_call(
        _ssm_discretize_kernel,
        out_shape=(ab_shape, ab_shape, bb_shape, bb_shape),
        name="ssm_discretize",
    )(a_re.reshape(ab_shape.shape), a_im.reshape(ab_shape.shape),
      log_dt.reshape(n_layers, N_GROUPS, 1, 1), b_re_t, b_im_t)


def _ssm_kernel(x_ref, g_ref, wb_ref, ab_re_ref, ab_im_ref, wc_re_ref, wc_im_ref, d_ref, wglu_ref,
                h0_re_ref, h0_im_ref, o_ref, hf_re_ref, hf_im_ref,
                s_re_ref, s_im_ref, st_re_ref, st_im_ref, tm_ref, abb_re_ref, abb_im_ref,
                *, n_batch, n_time):
    step = pl.program_id(0)
    blk = SSM_GROUP_BLOCK * STATE
    cblk = SSM_GROUP_BLOCK * GROUP
    lane_tiles = [slice(j * V7X_LANES, (j + 1) * V7X_LANES) for j in range(D_MODEL // V7X_LANES)]

    @pl.when(step == 0)
    def _():
        st_re_ref[...] = h0_re_ref[...]
        st_im_ref[...] = h0_im_ref[...]
        abb_re_ref[...] = jnp.broadcast_to(ab_re_ref[...], abb_re_ref.shape)
        abb_im_ref[...] = jnp.broadcast_to(ab_im_ref[...], abb_im_ref.shape)

    if n_time == 1:
        u = _rmsnorm(x_ref[...], g_ref[...])
    else:
        for b in range(n_batch):
            ub_rows = _rmsnorm(x_ref[b], g_ref[...])
            for j, cols in enumerate(lane_tiles):
                tm_ref[j, pl.ds(b, n_time, stride=n_batch), :] = ub_rows[:, cols]
        u = jnp.concatenate([tm_ref[j] for j in range(len(lane_tiles))], axis=1)
    ub = u.astype(BF16)

    def project_in(k):
        bu = _dot(ub[:, k * cblk:(k + 1) * cblk], wb_ref[k])
        s_re_ref[:, k * blk:(k + 1) * blk] = bu[:, :blk]
        s_im_ref[:, k * blk:(k + 1) * blk] = bu[:, blk:]

    def scan(c):
        lanes = slice(c * SCAN_LANES, (c + 1) * SCAN_LANES)
        ar = abb_re_ref[:, lanes]
        ai = abb_im_ref[:, lanes]
        for s in range(n_batch // V7X_SUBLANES):
            rows0 = slice(s * V7X_SUBLANES, (s + 1) * V7X_SUBLANES)
            hr = st_re_ref[rows0, lanes]
            hi = st_im_ref[rows0, lanes]
            for t in range(n_time):
                rows = slice(t * n_batch + s * V7X_SUBLANES, t * n_batch + (s + 1) * V7X_SUBLANES)
                hr, hi = (ar * hr - ai * hi + s_re_ref[rows, lanes],
                          ar * hi + ai * hr + s_im_ref[rows, lanes])
                s_re_ref[rows, lanes] = hr
                s_im_ref[rows, lanes] = hi
            st_re_ref[rows0, lanes] = hr
            st_im_ref[rows0, lanes] = hi

    def project_out(k):
        hr = s_re_ref[:, k * blk:(k + 1) * blk].astype(BF16)
        hi = s_im_ref[:, k * blk:(k + 1) * blk].astype(BF16)
        return _dot(hr, wc_re_ref[k]) - _dot(hi, wc_im_ref[k])

    ys = []
    if n_time == 1:
        for k in range(SSM_BLOCKS):
            project_in(k)
        ar = ab_re_ref[...]
        ai = ab_im_ref[...]
        hr0 = st_re_ref[...]
        hi0 = st_im_ref[...]
        hr = ar * hr0 - ai * hi0 + s_re_ref[...]
        hi = ar * hi0 + ai * hr0 + s_im_ref[...]
        s_re_ref[...] = hr
        s_im_ref[...] = hi
        st_re_ref[...] = hr
        st_im_ref[...] = hi
        for k in range(SSM_BLOCKS):
            ys.append(project_out(k))
    else:
        chunks_per_block = blk // SCAN_LANES
        project_in(0)
        for k in range(SSM_BLOCKS):
            if k + 1 < SSM_BLOCKS:
                project_in(k + 1)
            for c in range(chunks_per_block):
                scan(k * chunks_per_block + c)
            ys.append(project_out(k))
    y = jnp.concatenate(ys, axis=-1) + d_ref[...] * u
    z = jax.nn.gelu(y).astype(BF16)
    vg = _dot(z, wglu_ref[...])
    out = vg[:, :D_MODEL] * jax.nn.sigmoid(vg[:, D_MODEL:])
    if n_time == 1:
        o_ref[...] = x_ref[...] + out
    else:
        for j, cols in enumerate(lane_tiles):
            tm_ref[j] = out[:, cols]
        for b in range(n_batch):
            rows = pl.ds(b, n_time, stride=n_batch)
            out_b = jnp.concatenate([tm_ref[j, rows, :] for j in range(len(lane_tiles))], axis=1)
            o_ref[b] = x_ref[b] + out_b

    @pl.when(step == pl.num_programs(0) - 1)
    def _():
        hf_re_ref[...] = st_re_ref[...]
        hf_im_ref[...] = st_im_ref[...]


def _ssm_mixer(x, g, wb, ab_re, ab_im, wc_re, wc_im, d_skip, w_glu, h0_re, h0_im,
               layer, j, n_batch, n_time_total):
    n_time = min(SSM_TIME_TILE, n_time_total)
    tile = n_time * n_batch
    if n_time_total == 1:
        x_in = x
        row_spec = _const_spec((n_batch, D_MODEL))
    else:
        x_in = x.reshape(n_batch, n_time_total, D_MODEL)
        row_spec = pl.BlockSpec((n_batch, n_time, D_MODEL), lambda i: (0, i, 0))
    state_spec = _const_spec((n_batch, SSM_LANES))
    state_shape = jax.ShapeDtypeStruct((n_batch, SSM_LANES), F32)
    blk = SSM_GROUP_BLOCK * STATE
    cblk = SSM_GROUP_BLOCK * GROUP
    out, h_re, h_im = pl.pallas_call(
        functools.partial(_ssm_kernel, n_batch=n_batch, n_time=n_time),
        out_shape=(jax.ShapeDtypeStruct(x_in.shape, F32), state_shape, state_shape),
        grid=(n_time_total // n_time,),
        in_specs=[row_spec,
                  _layer_spec((1, D_MODEL), layer),
                  _layer_spec((SSM_BLOCKS, cblk, 2 * blk), j),
                  _layer_spec((1, SSM_LANES), j),
                  _layer_spec((1, SSM_LANES), j),
                  _layer_spec((SSM_BLOCKS, blk, cblk), j),
                  _layer_spec((SSM_BLOCKS, blk, cblk), j),
                  _layer_spec((1, D_MODEL), j),
                  _layer_spec((D_MODEL, 2 * D_MODEL), j),
                  state_spec, state_spec],
        out_specs=(row_spec, state_spec, state_spec),
        scratch_shapes=[pltpu.VMEM((tile, SSM_LANES), F32), pltpu.VMEM((tile, SSM_LANES), F32),
                        pltpu.VMEM((n_batch, SSM_LANES), F32), pltpu.VMEM((n_batch, SSM_LANES), F32),
                        pltpu.VMEM((D_MODEL // V7X_LANES, tile, V7X_LANES), F32),
                        pltpu.VMEM((V7X_SUBLANES, SSM_LANES), F32),
                        pltpu.VMEM((V7X_SUBLANES, SSM_LANES), F32)],
        compiler_params=_params("arbitrary"),
        name="ssm_mixer",
    )(x_in, g, wb, ab_re, ab_im, wc_re, wc_im, d_skip, w_glu, h0_re, h0_im)
    return out.reshape(x.shape), h_re, h_im


def _block_diag(w, rows_inner, cols_inner):
    n_layers = w.shape[0]
    gb = SSM_GROUP_BLOCK
    w = w.reshape(n_layers, SSM_BLOCKS, gb, rows_inner, cols_inner)
    eye = jnp.eye(gb, dtype=w.dtype)
    out = jnp.einsum('lkgrc,gh->lkgrhc', w, eye)
    return out.reshape(n_layers, SSM_BLOCKS, gb * rows_inner, gb * cols_inner)


def _mem_kv_kernel(m_ref, g_ref, wk_ref, wv_ref, k_ref, v_ref):
    m = _rmsnorm(m_ref[...], g_ref[...]).astype(BF16)
    k_ref[...] = _dot(m, wk_ref[...])
    v_ref[...] = _dot(m, wv_ref[...])


def _mem_kv(mem, g, wk, wv):
    rows = mem.shape[0]
    tm = ROW_TILE
    out = jax.ShapeDtypeStruct((DEPTH, rows, D_MODEL), F32)
    w_spec = pl.BlockSpec((None, D_MODEL, D_MODEL), lambda l, i: (l, 0, 0))
    o_spec = pl.BlockSpec((None, tm, D_MODEL), lambda l, i: (l, i, 0))
    return pl.pallas_call(
        _mem_kv_kernel,
        out_shape=(out, out),
        grid=(DEPTH, rows // tm),
        in_specs=[pl.BlockSpec((tm, D_MODEL), lambda l, i: (i, 0)),
                  pl.BlockSpec((None, 1, D_MODEL), lambda l, i: (l, 0, 0)),
                  w_spec, w_spec],
        out_specs=(o_spec, o_spec),
        compiler_params=_params("parallel", "parallel"),
        name="mem_kv",
    )(mem, g, wk, wv)


def _softmax_rows(s):
    e = jnp.exp(s - jnp.max(s, axis=-1, keepdims=True))
    return e / jnp.sum(e, axis=-1, keepdims=True)


def _attn_kernel(x_ref, xs_ref, g_ref, wq_ref, k_ref, v_ref, ck_ref, cv_ref, wo_ref, o_ref, os_ref,
                 kb_ref, vb_ref, oh_ref, qs_ref, ohs_ref, *, nb):
    b, t = pl.program_id(0), pl.program_id(1)
    step = b * pl.num_programs(1) + t
    last = pl.num_programs(0) * pl.num_programs(1) - 1
    scale = X_HEAD_DIM ** -0.5
    head_cols = [slice(h * X_HEAD_DIM, (h + 1) * X_HEAD_DIM) for h in range(X_HEADS)]

    @pl.when(step == 0)
    def _():
        hs = _rmsnorm(xs_ref[...], g_ref[...]).astype(BF16)
        qs_ref[...] = _dot(hs, wq_ref[...])
        ohs_ref[...] = jnp.zeros_like(ohs_ref)

    @pl.when(t == 0)
    def _():
        kb_ref[...] = k_ref[...].astype(BF16)
        vb_ref[...] = v_ref[...].astype(BF16)

    x = x_ref[...]
    h = _rmsnorm(x, g_ref[...]).astype(BF16)
    q = _dot(h, wq_ref[...])
    for cols in head_cols:
        s = lax.dot_general(q[:, cols].astype(BF16), kb_ref[:, cols], (((1,), (1,)), ((), ())),
                            preferred_element_type=F32) * scale
        p = _softmax_rows(s).astype(BF16)
        oh_ref[:, cols] = _dot(p, vb_ref[:, cols]).astype(BF16)
    o_ref[...] = x + _dot(oh_ref[...], wo_ref[...])

    parts = V7X_SUBLANES // nb
    sub = step % parts
    row0 = pl.multiple_of((step // parts) * V7X_SUBLANES, V7X_SUBLANES)
    q_group = qs_ref[pl.ds(row0, V7X_SUBLANES), :]
    q_rows = q_group[:nb, :]
    for c in range(1, parts):
        q_rows = jnp.where(sub == c, q_group[c * nb:(c + 1) * nb, :], q_rows)
    n_rows = N_MEM * X_HEADS
    head_of_lane = lax.broadcasted_iota(jnp.int32, (V7X_SUBLANES, n_rows), 1) % X_HEADS
    head_of_row = lax.broadcasted_iota(jnp.int32, (V7X_SUBLANES, n_rows), 0) % X_HEADS
    own_head = head_of_lane == head_of_row
    outs = []
    for r in range(nb):
        qr = q_rows[r:r + 1, :]
        qh = jnp.concatenate([qr[:, c] for c in head_cols] * (V7X_SUBLANES // X_HEADS), axis=0)
        k2 = ck_ref[r].reshape(n_rows, X_HEAD_DIM).astype(BF16)
        v2 = cv_ref[r].reshape(n_rows, X_HEAD_DIM).astype(BF16)
        sc = lax.dot_general(qh.astype(BF16), k2, (((1,), (1,)), ((), ())),
                             preferred_element_type=F32) * scale
        p = _softmax_rows(jnp.where(own_head, sc, -jnp.inf)).astype(BF16)
        pv = _dot(p, v2)
        outs.append(jnp.concatenate([pv[hd:hd + 1, :] for hd in range(X_HEADS)], axis=1))
    o_group = jnp.concatenate(outs * parts, axis=0)
    part_of_row = lax.broadcasted_iota(jnp.int32, (V7X_SUBLANES, D_MODEL), 0) // nb
    rows = pl.ds(row0, V7X_SUBLANES)
    ohs_ref[rows, :] = jnp.where(part_of_row == sub, o_group, ohs_ref[rows, :])

    @pl.when(step == last)
    def _():
        os_ref[...] = xs_ref[...] + _dot(ohs_ref[...].astype(BF16), wo_ref[...])


def _attn(x, xs, g, wq, k, v, cache_k, cache_v, wo, layer, n_seq, seq_len):
    tm = ROW_TILE
    n_t = seq_len // tm
    nb, rem = divmod(xs.shape[0], n_seq * n_t)
    assert rem == 0 and V7X_SUBLANES % nb == 0
    row_spec = pl.BlockSpec((tm, D_MODEL), lambda b, i: (b * n_t + i, 0))
    kv_spec = pl.BlockSpec((None, None, N_MEM, D_MODEL), lambda b, i: (layer, b, 0, 0))
    cache_spec = pl.BlockSpec((None, nb, N_MEM, X_HEADS, X_HEAD_DIM),
                              lambda b, i: (layer, b * n_t + i, 0, 0, 0))
    return pl.pallas_call(
        functools.partial(_attn_kernel, nb=nb),
        out_shape=(jax.ShapeDtypeStruct(x.shape, F32), jax.ShapeDtypeStruct(xs.shape, F32)),
        grid=(n_seq, n_t),
        in_specs=[row_spec,
                  _const_spec(xs.shape),
                  _layer_spec((1, D_MODEL), layer),
                  _layer_spec((D_MODEL, D_MODEL), layer),
                  kv_spec, kv_spec, cache_spec, cache_spec,
                  _layer_spec((D_MODEL, D_MODEL), layer)],
        out_specs=(row_spec, _const_spec(xs.shape)),
        scratch_shapes=[pltpu.VMEM((N_MEM, D_MODEL), BF16), pltpu.VMEM((N_MEM, D_MODEL), BF16),
                        pltpu.VMEM((tm, D_MODEL), BF16),
                        pltpu.VMEM(xs.shape, F32), pltpu.VMEM(xs.shape, F32)],
        compiler_params=_params("arbitrary", "arbitrary"),
        name="attn",
    )(x, xs, g, wq, k, v, cache_k, cache_v, wo)


def _trunk(x, xs, n_seq, seq_len, mem_k, mem_v, cache_k, cache_v, state_re, state_im, w):
    n_samples = xs.shape[0]
    zero_state = jnp.zeros((n_seq, SSM_LANES), F32)
    p_re, p_im, s_re, s_im, s_v = [], [], [], [], []
    for i in range(DEPTH):
        j = i // 2
        x, xs = _ffn(x, xs, w['norm_ffn1'], w['ffn1_wg'], w['ffn1_wu'], w['ffn1_wd'], w['norm_final'],
                     i, False)
        if i % 2 == 0:
            x, xs, v_rows = _mixer_a(x, xs, w['norm_mix'], w['a_w_in'], w['a_ln_g'], w['a_ln_b'],
                                     w['a_w_s'], w['a_bias_full'], w['a_scale0'], w['a_bias0'],
                                     w['a_w_out'], i, j)
            s_v.append(v_rows)
        else:
            ssm_w = (w['norm_mix'], w['b_wb'], w['b_ab_re'], w['b_ab_im'], w['b_wc_re'], w['b_wc_im'],
                     w['b_d'], w['b_w_glu'])
            x, hr, hi = _ssm_mixer(x, *ssm_w, zero_state, zero_state, i, j, n_seq, seq_len)
            p_re.append(hr)
            p_im.append(hi)
            xs, hr, hi = _ssm_mixer(xs, *ssm_w, state_re[j], state_im[j], i, j, n_samples, 1)
            s_re.append(hr)
            s_im.append(hi)
        x, xs = _attn(x, xs, w['norm_x'], w['x_wq'], mem_k, mem_v, cache_k, cache_v, w['x_wo'],
                      i, n_seq, seq_len)
        x, xs = _ffn(x, xs, w['norm_ffn2'], w['ffn2_wg'], w['ffn2_wu'], w['ffn2_wd'], w['norm_final'],
                     i, i == DEPTH - 1)
    return x, xs, p_re, p_im, s_re, s_im, s_v


def kernel(x_prompt, x_sample, cache_mem_k, cache_mem_v, state_ssm_re, state_ssm_im, mem_prompt,
           norm_ffn1, ffn1_wg, ffn1_wu, ffn1_wd, norm_mix,
           a_w_in, a_ln_g, a_ln_b, a_w_s, a_b_s, a_w_out,
           b_a_re, b_a_im, b_log_dt, b_b_re, b_b_im, b_c_re, b_c_im, b_d, b_w_glu,
           norm_x, norm_mem, x_wq, x_wk, x_wv, x_wo,
           norm_ffn2, ffn2_wg, ffn2_wu, ffn2_wd, norm_final):
    batch, seq, _ = x_prompt.shape
    dec_batch, dec_seq, _ = x_sample.shape
    assert dec_seq == 1 and seq % ROW_TILE == 0 and ROW_TILE % CHUNK == 0
    n_a = a_w_in.shape[0]
    n_b = b_a_re.shape[0]

    def vec(p):
        return p.reshape(p.shape[0], 1, p.shape[-1])

    ab_re, ab_im, bb_re, bb_im = _ssm_discretize(
        b_a_re, b_a_im, b_log_dt, b_b_re.transpose(0, 1, 3, 2), b_b_im.transpose(0, 1, 3, 2))
    wb = jnp.concatenate([_block_diag(bb_re, GROUP, STATE), _block_diag(bb_im, GROUP, STATE)],
                         axis=-1).astype(BF16)
    w = dict(
        norm_ffn1=vec(norm_ffn1), norm_ffn2=vec(norm_ffn2), norm_mix=vec(norm_mix), norm_x=vec(norm_x),
        norm_final=norm_final.reshape(1, D_MODEL),
        ffn1_wg=ffn1_wg.astype(BF16), ffn1_wu=ffn1_wu.astype(BF16), ffn1_wd=ffn1_wd.astype(BF16),
        ffn2_wg=ffn2_wg.astype(BF16), ffn2_wu=ffn2_wu.astype(BF16), ffn2_wd=ffn2_wd.astype(BF16),
        a_w_in=a_w_in.astype(BF16), a_w_out=a_w_out.astype(BF16),
        a_ln_g=vec(a_ln_g), a_ln_b=vec(a_ln_b), a_w_s=a_w_s,
        a_bias_full=jnp.repeat(jnp.swapaxes(a_b_s, 1, 2), A_HEAD_DIM, axis=-1),
        a_scale0=jnp.repeat(a_w_s[:, :, 0, 0], A_HEAD_DIM, axis=-1).reshape(n_a, 1, D_A),
        a_bias0=jnp.repeat(a_b_s[:, :, 0], A_HEAD_DIM, axis=-1).reshape(n_a, 1, D_A),
        b_wb=wb,
        b_ab_re=ab_re.reshape(n_b, 1, SSM_LANES), b_ab_im=ab_im.reshape(n_b, 1, SSM_LANES),
        b_wc_re=_block_diag(b_c_re.transpose(0, 1, 3, 2), STATE, GROUP).astype(BF16),
        b_wc_im=_block_diag(b_c_im.transpose(0, 1, 3, 2), STATE, GROUP).astype(BF16),
        b_d=b_d.reshape(n_b, 1, D_MODEL), b_w_glu=b_w_glu.astype(BF16),
        x_wq=x_wq.astype(BF16), x_wo=x_wo.astype(BF16),
    )

    mem_k, mem_v = _mem_kv(mem_prompt.reshape(batch * N_MEM, D_MODEL), vec(norm_mem),
                           x_wk.astype(BF16), x_wv.astype(BF16))
    mem_k = mem_k.reshape(DEPTH, batch, N_MEM, D_MODEL)
    mem_v = mem_v.reshape(DEPTH, batch, N_MEM, D_MODEL)
    y_prompt, y_sample, p_re, p_im, s_re, s_im, s_v = _trunk(
        x_prompt.reshape(batch * seq, D_MODEL), x_sample.reshape(dec_batch, D_MODEL), batch, seq,
        mem_k, mem_v, cache_mem_k, cache_mem_v,
        state_ssm_re.reshape(n_b, dec_batch, SSM_LANES),
        state_ssm_im.reshape(n_b, dec_batch, SSM_LANES), w)

    def states(parts, n):
        return jnp.stack(parts).reshape(n_b, n, N_GROUPS, STATE)

    kv_shape = (DEPTH, batch, N_MEM, X_HEADS, X_HEAD_DIM)
    return (y_prompt.reshape(batch, seq, D_MODEL),
            y_sample.reshape(dec_batch, 1, D_MODEL),
            mem_k.reshape(kv_shape), mem_v.reshape(kv_shape),
            states(p_re, batch), states(p_im, batch),
            states(s_re, dec_batch), states(s_im, dec_batch),
            jnp.stack(s_v).reshape(n_a, dec_batch, 1, D_A))
```

```python
import functools
import math

import jax
import jax.numpy as jnp
from jax import lax
from jax.experimental import pallas as pl
from jax.experimental.pallas import tpu as pltpu

F32 = jnp.float32
BF16 = jnp.bfloat16

D_MODEL = 1024
DEPTH = 4
CHUNK = 128
D_A = 2 * D_MODEL
A_HEADS = 8
A_HEAD_DIM = D_A // A_HEADS
GROUP = 16
N_GROUPS = D_MODEL // GROUP
STATE = 64
SSM_LANES = N_GROUPS * STATE
N_MEM = 256
X_HEADS = 4
X_HEAD_DIM = D_MODEL // X_HEADS
D_FF = 2816
EPS = 1e-6

V7X_SUBLANES = 8
V7X_LANES = 128
V7X_VMEM_LIMIT_BYTES = 56 * 1024 * 1024

ROW_TILE = 512
FFN_ROW_TILE = 1024
FFN_FF_CHUNK = 768
SSM_TIME_TILE = 64
SSM_GROUP_BLOCK = 16
SSM_BLOCKS = N_GROUPS // SSM_GROUP_BLOCK
SCAN_LANES = 512
MEM_KV_SEQS = 2


def _params(*semantics):
    return pltpu.CompilerParams(dimension_semantics=semantics,
                                vmem_limit_bytes=V7X_VMEM_LIMIT_BYTES)


def _const_spec(shape):
    nd = len(shape)
    return pl.BlockSpec(shape, lambda *_: (0,) * nd, pipeline_mode=pl.Buffered(1))


def _layer_spec(shape, layer):
    nd = len(shape)
    return pl.BlockSpec((None,) + tuple(shape), lambda *_: (layer,) + (0,) * nd,
                        pipeline_mode=pl.Buffered(1))


def _rmsnorm(x, g):
    return x * lax.rsqrt(jnp.mean(x * x, axis=-1, keepdims=True) + EPS) * g


def _dot(a, b):
    return jnp.dot(a, b, preferred_element_type=F32)


def _ffn_kernel(x_ref, xs_ref, g_ref, wg_ref, wu_ref, wd_ref, gf_ref, o_ref, os_ref, *, final_norm):
    def ffn_rows(x):
        h = _rmsnorm(x, g_ref[...]).astype(BF16)
        down = None
        for c0 in range(0, D_FF, FFN_FF_CHUNK):
            cols = slice(c0, min(c0 + FFN_FF_CHUNK, D_FF))
            gate = _dot(h, wg_ref[:, cols])
            up = _dot(h, wu_ref[:, cols])
            act = (jax.nn.silu(gate) * up).astype(BF16)
            part = _dot(act, wd_ref[cols, :])
            down = part if down is None else down + part
        y = x + 0.5 * down
        return _rmsnorm(y, gf_ref[...]) if final_norm else y

    o_ref[...] = ffn_rows(x_ref[...])

    @pl.when(pl.program_id(0) == pl.num_programs(0) - 1)
    def _():
        os_ref[...] = ffn_rows(xs_ref[...])


def _ffn(x, xs, g, wg, wu, wd, g_final, layer, final_norm):
    rows = x.shape[0]
    tm = FFN_ROW_TILE
    row_spec = pl.BlockSpec((tm, D_MODEL), lambda i: (i, 0))
    return pl.pallas_call(
        functools.partial(_ffn_kernel, final_norm=final_norm),
        out_shape=(jax.ShapeDtypeStruct(x.shape, F32), jax.ShapeDtypeStruct(xs.shape, F32)),
        grid=(rows // tm,),
        in_specs=[row_spec,
                  _const_spec(xs.shape),
                  _layer_spec((1, D_MODEL), layer),
                  _layer_spec((D_MODEL, D_FF), layer),
                  _layer_spec((D_MODEL, D_FF), layer),
                  _layer_spec((D_FF, D_MODEL), layer),
                  _const_spec((1, D_MODEL))],
        out_specs=(row_spec, _const_spec(xs.shape)),
        compiler_params=_params("arbitrary"),
        name="ffn",
    )(x, xs, g, wg, wu, wd, g_final)


def _gated_unit_inputs(x_ref, g_ref, w_in_ref, ln_g_ref, ln_b_ref):
    h = _rmsnorm(x_ref[...], g_ref[...]).astype(BF16)
    uv = jax.nn.gelu(_dot(h, w_in_ref[...]))
    u = uv[:, :D_A]
    v = uv[:, D_A:]
    mu = jnp.mean(v, axis=-1, keepdims=True)
    vc = v - mu
    var = jnp.mean(vc * vc, axis=-1, keepdims=True)
    v = vc * lax.rsqrt(var + EPS) * ln_g_ref[...] + ln_b_ref[...]
    return u, v


def _mixer_a_kernel(x_ref, xs_ref, g_ref, w_in_ref, ln_g_ref, ln_b_ref, ws_ref, bias_ref,
                    scale0_ref, bias0_ref, w_out_ref, o_ref, os_ref, vs_ref, gated_ref):
    u, v = _gated_unit_inputs(x_ref, g_ref, w_in_ref, ln_g_ref, ln_b_ref)
    vb = v.astype(BF16)
    t_idx = lax.broadcasted_iota(jnp.int32, (CHUNK, CHUNK), 0)
    s_idx = lax.broadcasted_iota(jnp.int32, (CHUNK, CHUNK), 1)
    causal = t_idx >= s_idx
    for head in range(A_HEADS):
        cols = slice(head * A_HEAD_DIM, (head + 1) * A_HEAD_DIM)
        ws = jnp.where(causal, ws_ref[head], 0.0).astype(BF16)
        bias = bias_ref[:, cols]
        for c in range(x_ref.shape[0] // CHUNK):
            rows = slice(c * CHUNK, (c + 1) * CHUNK)
            mixed = _dot(ws, vb[rows, cols]) + bias
            gated_ref[rows, cols] = (u[rows, cols] * mixed).astype(BF16)
    o_ref[...] = x_ref[...] + _dot(gated_ref[...], w_out_ref[...])

    @pl.when(pl.program_id(0) == pl.num_programs(0) - 1)
    def _():
        us, vs = _gated_unit_inputs(xs_ref, g_ref, w_in_ref, ln_g_ref, ln_b_ref)
        vs_ref[...] = vs
        mixed = vs * scale0_ref[...] + bias0_ref[...]
        os_ref[...] = xs_ref[...] + _dot((us * mixed).astype(BF16), w_out_ref[...])


def _mixer_a(x, xs, g, w_in, ln_g, ln_b, w_s, bias_full, scale0, bias0, w_out, layer, j):
    rows = x.shape[0]
    tm = ROW_TILE
    row_spec = pl.BlockSpec((tm, D_MODEL), lambda i: (i, 0))
    vs_shape = (xs.shape[0], D_A)
    return pl.pallas_call(
        _mixer_a_kernel,
        out_shape=(jax.ShapeDtypeStruct(x.shape, F32), jax.ShapeDtypeStruct(xs.shape, F32),
                   jax.ShapeDtypeStruct(vs_shape, F32)),
        grid=(rows // tm,),
        in_specs=[row_spec,
                  _const_spec(xs.shape),
                  _layer_spec((1, D_MODEL), layer),
                  _layer_spec((D_MODEL, 2 * D_A), j),
                  _layer_spec((1, D_A), j),
                  _layer_spec((1, D_A), j),
                  _layer_spec((A_HEADS, CHUNK, CHUNK), j),
                  _layer_spec((CHUNK, D_A), j),
                  _layer_spec((1, D_A), j),
                  _layer_spec((1, D_A), j),
                  _layer_spec((D_A, D_MODEL), j)],
        out_specs=(row_spec, _const_spec(xs.shape), _const_spec(vs_shape)),
        scratch_shapes=[pltpu.VMEM((tm, D_A), BF16)],
        compiler_params=_params("arbitrary"),
        name="mixer_a",
    )(x, xs, g, w_in, ln_g, ln_b, w_s, bias_full, scale0, bias0, w_out)


def _ssm_discretize_kernel(a_re_ref, a_im_ref, log_dt_ref, b_re_ref, b_im_ref,
                           ab_re_ref, ab_im_ref, bb_re_ref, bb_im_ref):
    a_re = a_re_ref[...]
    a_im = a_im_ref[...]
    dt = jnp.exp(log_dt_ref[...])
    mag = jnp.exp(a_re * dt)
    ab_re = mag * jnp.cos(a_im * dt)
    ab_im = mag * jnp.sin(a_im * dt)
    den = a_re * a_re + a_im * a_im
    nr = ab_re - 1.0
    ni = ab_im
    f_re = (nr * a_re + ni * a_im) / den
    f_im = (ni * a_re - nr * a_im) / den
    b_re = b_re_ref[...]
    b_im = b_im_ref[...]
    ab_re_ref[...] = ab_re
    ab_im_ref[...] = ab_im
    bb_re_ref[...] = f_re * b_re - f_im * b_im
    bb_im_ref[...] = f_re * b_im + f_im * b_re


def _ssm_discretize(a_re, a_im, log_dt, b_re_t, b_im_t):
    n_layers = a_re.shape[0]
    ab_shape = jax.ShapeDtypeStruct((n_layers, N_GROUPS, 1, STATE), F32)
    bb_shape = jax.ShapeDtypeStruct((n_layers, N_GROUPS, GROUP, STATE), F32)
    return pl.pallas_call(
        _ssm_discretize_kernel,
        out_shape=(ab_shape, ab_shape, bb_shape, bb_shape),
        name="ssm_discretize",
    )(a_re.reshape(ab_shape.shape), a_im.reshape(ab_shape.shape),
      log_dt.reshape(n_layers, N_GROUPS, 1, 1), b_re_t, b_im_t)


def _ssm_kernel(x_ref, g_ref, wb_ref, ab_re_ref, ab_im_ref, wc_re_ref, wc_im_ref, d_ref, wglu_ref,
                h0_re_ref, h0_im_ref, o_ref, hf_re_ref, hf_im_ref,
                s_re_ref, s_im_ref, st_re_ref, st_im_ref, tm_ref, abb_re_ref, abb_im_ref,
                *, n_batch, n_time):
    step = pl.program_id(0)
    blk = SSM_GROUP_BLOCK * STATE
    cblk = SSM_GROUP_BLOCK * GROUP
    lane_tiles = [slice(j * V7X_LANES, (j + 1) * V7X_LANES) for j in range(D_MODEL // V7X_LANES)]

    @pl.when(step == 0)
    def _():
        st_re_ref[...] = h0_re_ref[...]
        st_im_ref[...] = h0_im_ref[...]
        abb_re_ref[...] = jnp.broadcast_to(ab_re_ref[...], abb_re_ref.shape)
        abb_im_ref[...] = jnp.broadcast_to(ab_im_ref[...], abb_im_ref.shape)

    if n_time == 1:
        u = _rmsnorm(x_ref[...], g_ref[...])
    else:
        for b in range(n_batch):
            ub_rows = _rmsnorm(x_ref[b], g_ref[...])
            for j, cols in enumerate(lane_tiles):
                tm_ref[j, pl.ds(b, n_time, stride=n_batch), :] = ub_rows[:, cols]
        u = jnp.concatenate([tm_ref[j] for j in range(len(lane_tiles))], axis=1)
    ub = u.astype(BF16)

    def project_in(k):
        bu = _dot(ub[:, k * cblk:(k + 1) * cblk], wb_ref[k])
        s_re_ref[:, k * blk:(k + 1) * blk] = bu[:, :blk]
        s_im_ref[:, k * blk:(k + 1) * blk] = bu[:, blk:]

    def scan(c):
        lanes = slice(c * SCAN_LANES, (c + 1) * SCAN_LANES)
        ar = abb_re_ref[:, lanes]
        ai = abb_im_ref[:, lanes]
        for s in range(n_batch // V7X_SUBLANES):
            rows0 = slice(s * V7X_SUBLANES, (s + 1) * V7X_SUBLANES)
            hr = st_re_ref[rows0, lanes]
            hi = st_im_ref[rows0, lanes]
            for t in range(n_time):
                rows = slice(t * n_batch + s * V7X_SUBLANES, t * n_batch + (s + 1) * V7X_SUBLANES)
                hr, hi = (ar * hr - ai * hi + s_re_ref[rows, lanes],
                          ar * hi + ai * hr + s_im_ref[rows, lanes])
                s_re_ref[rows, lanes] = hr
                s_im_ref[rows, lanes] = hi
            st_re_ref[rows0, lanes] = hr
            st_im_ref[rows0, lanes] = hi

    def project_out(k):
        hr = s_re_ref[:, k * blk:(k + 1) * blk].astype(BF16)
        hi = s_im_ref[:, k * blk:(k + 1) * blk].astype(BF16)
        return _dot(hr, wc_re_ref[k]) - _dot(hi, wc_im_ref[k])

    ys = []
    if n_time == 1:
        for k in range(SSM_BLOCKS):
            project_in(k)
        ar = ab_re_ref[...]
        ai = ab_im_ref[...]
        hr0 = st_re_ref[...]
        hi0 = st_im_ref[...]
        hr = ar * hr0 - ai * hi0 + s_re_ref[...]
        hi = ar * hi0 + ai * hr0 + s_im_ref[...]
        s_re_ref[...] = hr
        s_im_ref[...] = hi
        st_re_ref[...] = hr
        st_im_ref[...] = hi
        for k in range(SSM_BLOCKS):
            ys.append(project_out(k))
    else:
        chunks_per_block = blk // SCAN_LANES
        project_in(0)
        for k in range(SSM_BLOCKS):
            if k + 1 < SSM_BLOCKS:
                project_in(k + 1)
            for c in range(chunks_per_block):
                scan(k * chunks_per_block + c)
            ys.append(project_out(k))
    y = jnp.concatenate(ys, axis=-1) + d_ref[...] * u
    z = jax.nn.gelu(y).astype(BF16)
    vg = _dot(z, wglu_ref[...])
    out = vg[:, :D_MODEL] * jax.nn.sigmoid(vg[:, D_MODEL:])
    if n_time == 1:
        o_ref[...] = x_ref[...] + out
    else:
        for j, cols in enumerate(lane_tiles):
            tm_ref[j] = out[:, cols]
        for b in range(n_batch):
            rows = pl.ds(b, n_time, stride=n_batch)
            out_b = jnp.concatenate([tm_ref[j, rows, :] for j in range(len(lane_tiles))], axis=1)
            o_ref[b] = x_ref[b] + out_b

    @pl.when(step == pl.num_programs(0) - 1)
    def _():
        hf_re_ref[...] = st_re_ref[...]
        hf_im_ref[...] = st_im_ref[...]


def _ssm_mixer(x, g, wb, ab_re, ab_im, wc_re, wc_im, d_skip, w_glu, h0_re, h0_im,
               layer, j, n_batch, n_time_total):
    n_time = min(SSM_TIME_TILE, n_time_total)
    tile = n_time * n_batch
    if n_time_total == 1:
        x_in = x
        row_spec = _const_spec((n_batch, D_MODEL))
    else:
        x_in = x.reshape(n_batch, n_time_total, D_MODEL)
        row_spec = pl.BlockSpec((n_batch, n_time, D_MODEL), lambda i: (0, i, 0))
    state_spec = _const_spec((n_batch, SSM_LANES))
    state_shape = jax.ShapeDtypeStruct((n_batch, SSM_LANES), F32)
    blk = SSM_GROUP_BLOCK * STATE
    cblk = SSM_GROUP_BLOCK * GROUP
    out, h_re, h_im = pl.pallas_call(
        functools.partial(_ssm_kernel, n_batch=n_batch, n_time=n_time),
        out_shape=(jax.ShapeDtypeStruct(x_in.shape, F32), state_shape, state_shape),
        grid=(n_time_total // n_time,),
        in_specs=[row_spec,
                  _layer_spec((1, D_MODEL), layer),
                  _layer_spec((SSM_BLOCKS, cblk, 2 * blk), j),
                  _layer_spec((1, SSM_LANES), j),
                  _layer_spec((1, SSM_LANES), j),
                  _layer_spec((SSM_BLOCKS, blk, cblk), j),
                  _layer_spec((SSM_BLOCKS, blk, cblk), j),
                  _layer_spec((1, D_MODEL), j),
                  _layer_spec((D_MODEL, 2 * D_MODEL), j),
                  state_spec, state_spec],
        out_specs=(row_spec, state_spec, state_spec),
        scratch_shapes=[pltpu.VMEM((tile, SSM_LANES), F32), pltpu.VMEM((tile, SSM_LANES), F32),
                        pltpu.VMEM((n_batch, SSM_LANES), F32), pltpu.VMEM((n_batch, SSM_LANES), F32),
                        pltpu.VMEM((D_MODEL // V7X_LANES, tile, V7X_LANES), F32),
                        pltpu.VMEM((V7X_SUBLANES, SSM_LANES), F32),
                        pltpu.VMEM((V7X_SUBLANES, SSM_LANES), F32)],
        compiler_params=_params("arbitrary"),
        name="ssm_mixer",
    )(x_in, g, wb, ab_re, ab_im, wc_re, wc_im, d_skip, w_glu, h0_re, h0_im)
    return out.reshape(x.shape), h_re, h_im


def _block_diag(w, rows_inner, cols_inner):
    n_layers = w.shape[0]
    gb = SSM_GROUP_BLOCK
    w = w.reshape(n_layers, SSM_BLOCKS, gb, rows_inner, 1, cols_inner)
    eye = jnp.eye(gb, dtype=w.dtype).reshape(1, 1, gb, 1, gb, 1)
    return (w * eye).reshape(n_layers, SSM_BLOCKS, gb * rows_inner, gb * cols_inner)


def _mem_kv_kernel(m_ref, g_ref, wk_ref, wv_ref, k_ref, v_ref, kb_ref, vb_ref):
    m = _rmsnorm(m_ref[...], g_ref[...]).astype(BF16)
    k = _dot(m, wk_ref[...])
    v = _dot(m, wv_ref[...])
    kb_ref[...] = k.astype(BF16)
    vb_ref[...] = v.astype(BF16)
    for s in range(MEM_KV_SEQS):
        rows = slice(s * N_MEM, (s + 1) * N_MEM)
        for h in range(X_HEADS):
            cols = slice(h * X_HEAD_DIM, (h + 1) * X_HEAD_DIM)
            k_ref[s, :, h, :] = k[rows, cols]
            v_ref[s, :, h, :] = v[rows, cols]


def _mem_kv(mem, g, wk, wv):
    n_seq = mem.shape[0]
    tm = MEM_KV_SEQS * N_MEM
    out = jax.ShapeDtypeStruct((DEPTH, n_seq, N_MEM, X_HEADS, X_HEAD_DIM), F32)
    out_b = jax.ShapeDtypeStruct((DEPTH, n_seq * N_MEM, D_MODEL), BF16)
    w_spec = pl.BlockSpec((None, D_MODEL, D_MODEL), lambda l, i: (l, 0, 0))
    o_spec = pl.BlockSpec((None, MEM_KV_SEQS, N_MEM, X_HEADS, X_HEAD_DIM), lambda l, i: (l, i, 0, 0, 0))
    ob_spec = pl.BlockSpec((None, tm, D_MODEL), lambda l, i: (l, i, 0))
    return pl.pallas_call(
        _mem_kv_kernel,
        out_shape=(out, out, out_b, out_b),
        grid=(DEPTH, n_seq // MEM_KV_SEQS),
        in_specs=[pl.BlockSpec((tm, D_MODEL), lambda l, i: (i, 0)),
                  pl.BlockSpec((None, 1, D_MODEL), lambda l, i: (l, 0, 0)),
                  w_spec, w_spec],
        out_specs=(o_spec, o_spec, ob_spec, ob_spec),
        compiler_params=_params("parallel", "parallel"),
        name="mem_kv",
    )(mem.reshape(n_seq * N_MEM, D_MODEL), g, wk, wv)


def _softmax_rows(s):
    e = jnp.exp(s - jnp.max(s, axis=-1, keepdims=True))
    return e / jnp.sum(e, axis=-1, keepdims=True)


def _attn_kernel(x_ref, xs_ref, g_ref, wq_ref, k_ref, v_ref, ck_ref, cv_ref, wo_ref, o_ref, os_ref,
                 oh_ref, qs_ref, ohs_ref, *, nb):
    b, t = pl.program_id(0), pl.program_id(1)
    step = b * pl.num_programs(1) + t
    last = pl.num_programs(0) * pl.num_programs(1) - 1
    scale = X_HEAD_DIM ** -0.5
    head_cols = [slice(h * X_HEAD_DIM, (h + 1) * X_HEAD_DIM) for h in range(X_HEADS)]

    @pl.when(step == 0)
    def _():
        hs = _rmsnorm(xs_ref[...], g_ref[...]).astype(BF16)
        qs_ref[...] = _dot(hs, wq_ref[...])
        ohs_ref[...] = jnp.zeros_like(ohs_ref)

    x = x_ref[...]
    h = _rmsnorm(x, g_ref[...]).astype(BF16)
    q = _dot(h, wq_ref[...])
    for cols in head_cols:
        s = lax.dot_general(q[:, cols].astype(BF16), k_ref[:, cols], (((1,), (1,)), ((), ())),
                            preferred_element_type=F32) * scale
        p = _softmax_rows(s).astype(BF16)
        oh_ref[:, cols] = _dot(p, v_ref[:, cols]).astype(BF16)
    o_ref[...] = x + _dot(oh_ref[...], wo_ref[...])

    parts = V7X_SUBLANES // nb
    sub = step % parts
    row0 = pl.multiple_of((step // parts) * V7X_SUBLANES, V7X_SUBLANES)
    q_group = qs_ref[pl.ds(row0, V7X_SUBLANES), :]
    q_rows = q_group[:nb, :]
    for c in range(1, parts):
        q_rows = jnp.where(sub == c, q_group[c * nb:(c + 1) * nb, :], q_rows)
    n_rows = N_MEM * X_HEADS
    head_of_lane = lax.broadcasted_iota(jnp.int32, (V7X_SUBLANES, n_rows), 1) % X_HEADS
    head_of_row = lax.broadcasted_iota(jnp.int32, (V7X_SUBLANES, n_rows), 0) % X_HEADS
    own_head = head_of_lane == head_of_row
    scores = []
    for r in range(nb):
        qr = q_rows[r:r + 1, :]
        qh = jnp.concatenate([qr[:, c] for c in head_cols] * (V7X_SUBLANES // X_HEADS), axis=0)
        k2 = ck_ref[r].reshape(n_rows, X_HEAD_DIM).astype(BF16)
        scores.append(lax.dot_general(qh.astype(BF16), k2, (((1,), (1,)), ((), ())),
                                      preferred_element_type=F32))
    sc = jnp.concatenate(scores, axis=0) * scale
    own = jnp.concatenate([own_head] * nb, axis=0)
    p_all = _softmax_rows(jnp.where(own, sc, -jnp.inf)).astype(BF16)
    outs = []
    for r in range(nb):
        v2 = cv_ref[r].reshape(n_rows, X_HEAD_DIM).astype(BF16)
        pv = _dot(p_all[r * V7X_SUBLANES:(r + 1) * V7X_SUBLANES, :], v2)
        outs.append(jnp.concatenate([pv[hd:hd + 1, :] for hd in range(X_HEADS)], axis=1))
    o_group = jnp.concatenate(outs * parts, axis=0)
    part_of_row = lax.broadcasted_iota(jnp.int32, (V7X_SUBLANES, D_MODEL), 0) // nb
    rows = pl.ds(row0, V7X_SUBLANES)
    ohs_ref[rows, :] = jnp.where(part_of_row == sub, o_group, ohs_ref[rows, :])

    @pl.when(step == last)
    def _():
        os_ref[...] = xs_ref[...] + _dot(ohs_ref[...].astype(BF16), wo_ref[...])


def _attn(x, xs, g, wq, k, v, cache_k, cache_v, wo, layer, n_seq, seq_len):
    tm = ROW_TILE
    n_t = seq_len // tm
    nb, rem = divmod(xs.shape[0], n_seq * n_t)
    assert rem == 0 and V7X_SUBLANES % nb == 0
    row_spec = pl.BlockSpec((tm, D_MODEL), lambda b, i: (b * n_t + i, 0))
    kv_spec = pl.BlockSpec((None, N_MEM, D_MODEL), lambda b, i: (layer, b, 0))
    cache_spec = pl.BlockSpec((None, nb, N_MEM, X_HEADS, X_HEAD_DIM),
                              lambda b, i: (layer, b * n_t + i, 0, 0, 0))
    return pl.pallas_call(
        functools.partial(_attn_kernel, nb=nb),
        out_shape=(jax.ShapeDtypeStruct(x.shape, F32), jax.ShapeDtypeStruct(xs.shape, F32)),
        grid=(n_seq, n_t),
        in_specs=[row_spec,
                  _const_spec(xs.shape),
                  _layer_spec((1, D_MODEL), layer),
                  _layer_spec((D_MODEL, D_MODEL), layer),
                  kv_spec, kv_spec, cache_spec, cache_spec,
                  _layer_spec((D_MODEL, D_MODEL), layer)],
        out_specs=(row_spec, _const_spec(xs.shape)),
        scratch_shapes=[pltpu.VMEM((tm, D_MODEL), BF16),
                        pltpu.VMEM(xs.shape, F32), pltpu.VMEM(xs.shape, F32)],
        compiler_params=_params("arbitrary", "arbitrary"),
        name="attn",
    )(x, xs, g, wq, k, v, cache_k, cache_v, wo)


def _trunk(x, xs, n_seq, seq_len, mem_k, mem_v, cache_k, cache_v, state_re, state_im, w):
    n_samples = xs.shape[0]
    zero_state = jnp.zeros((n_seq, SSM_LANES), F32)
    p_re, p_im, s_re, s_im, s_v = [], [], [], [], []
    for i in range(DEPTH):
        j = i // 2
        x, xs = _ffn(x, xs, w['norm_ffn1'], w['ffn1_wg'], w['ffn1_wu'], w['ffn1_wd'], w['norm_final'],
                     i, False)
        if i % 2 == 0:
            x, xs, v_rows = _mixer_a(x, xs, w['norm_mix'], w['a_w_in'], w['a_ln_g'], w['a_ln_b'],
                                     w['a_w_s'], w['a_bias_full'], w['a_scale0'], w['a_bias0'],
                                     w['a_w_out'], i, j)
            s_v.append(v_rows)
        else:
            ssm_w = (w['norm_mix'], w['b_wb'], w['b_ab_re'], w['b_ab_im'], w['b_wc_re'], w['b_wc_im'],
                     w['b_d'], w['b_w_glu'])
            x, hr, hi = _ssm_mixer(x, *ssm_w, zero_state, zero_state, i, j, n_seq, seq_len)
            p_re.append(hr)
            p_im.append(hi)
            xs, hr, hi = _ssm_mixer(xs, *ssm_w, state_re[j], state_im[j], i, j, n_samples, 1)
            s_re.append(hr)
            s_im.append(hi)
        x, xs = _attn(x, xs, w['norm_x'], w['x_wq'], mem_k, mem_v, cache_k, cache_v, w['x_wo'],
                      i, n_seq, seq_len)
        x, xs = _ffn(x, xs, w['norm_ffn2'], w['ffn2_wg'], w['ffn2_wu'], w['ffn2_wd'], w['norm_final'],
                     i, i == DEPTH - 1)
    return x, xs, p_re, p_im, s_re, s_im, s_v


def kernel(x_prompt, x_sample, cache_mem_k, cache_mem_v, state_ssm_re, state_ssm_im, mem_prompt,
           norm_ffn1, ffn1_wg, ffn1_wu, ffn1_wd, norm_mix,
           a_w_in, a_ln_g, a_ln_b, a_w_s, a_b_s, a_w_out,
           b_a_re, b_a_im, b_log_dt, b_b_re, b_b_im, b_c_re, b_c_im, b_d, b_w_glu,
           norm_x, norm_mem, x_wq, x_wk, x_wv, x_wo,
           norm_ffn2, ffn2_wg, ffn2_wu, ffn2_wd, norm_final):
    batch, seq, _ = x_prompt.shape
    dec_batch, dec_seq, _ = x_sample.shape
    assert dec_seq == 1 and seq % ROW_TILE == 0 and ROW_TILE % CHUNK == 0
    n_a = a_w_in.shape[0]
    n_b = b_a_re.shape[0]

    def vec(p):
        return p.reshape(p.shape[0], 1, p.shape[-1])

    ab_re, ab_im, bb_re, bb_im = _ssm_discretize(
        b_a_re, b_a_im, b_log_dt, b_b_re.transpose(0, 1, 3, 2), b_b_im.transpose(0, 1, 3, 2))
    wb = jnp.concatenate([_block_diag(bb_re.astype(BF16), GROUP, STATE),
                          _block_diag(bb_im.astype(BF16), GROUP, STATE)], axis=-1)
    w = dict(
        norm_ffn1=vec(norm_ffn1), norm_ffn2=vec(norm_ffn2), norm_mix=vec(norm_mix), norm_x=vec(norm_x),
        norm_final=norm_final.reshape(1, D_MODEL),
        ffn1_wg=ffn1_wg.astype(BF16), ffn1_wu=ffn1_wu.astype(BF16), ffn1_wd=ffn1_wd.astype(BF16),
        ffn2_wg=ffn2_wg.astype(BF16), ffn2_wu=ffn2_wu.astype(BF16), ffn2_wd=ffn2_wd.astype(BF16),
        a_w_in=a_w_in.astype(BF16), a_w_out=a_w_out.astype(BF16),
        a_ln_g=vec(a_ln_g), a_ln_b=vec(a_ln_b), a_w_s=a_w_s,
        a_bias_full=jnp.repeat(jnp.swapaxes(a_b_s, 1, 2), A_HEAD_DIM, axis=-1),
        a_scale0=jnp.repeat(a_w_s[:, :, 0, 0], A_HEAD_DIM, axis=-1).reshape(n_a, 1, D_A),
        a_bias0=jnp.repeat(a_b_s[:, :, 0], A_HEAD_DIM, axis=-1).reshape(n_a, 1, D_A),
        b_wb=wb,
        b_ab_re=ab_re.reshape(n_b, 1, SSM_LANES), b_ab_im=ab_im.reshape(n_b, 1, SSM_LANES),
        b_wc_re=_block_diag(b_c_re.transpose(0, 1, 3, 2).astype(BF16), STATE, GROUP),
        b_wc_im=_block_diag(b_c_im.transpose(0, 1, 3, 2).astype(BF16), STATE, GROUP),
        b_d=b_d.reshape(n_b, 1, D_MODEL), b_w_glu=b_w_glu.astype(BF16),
        x_wq=x_wq.astype(BF16), x_wo=x_wo.astype(BF16),
    )

    mem_k, mem_v, mem_kb, mem_vb = _mem_kv(mem_prompt, vec(norm_mem), x_wk.astype(BF16), x_wv.astype(BF16))
    y_prompt, y_sample, p_re, p_im, s_re, s_im, s_v = _trunk(
        x_prompt.reshape(batch * seq, D_MODEL), x_sample.reshape(dec_batch, D_MODEL), batch, seq,
        mem_kb, mem_vb, cache_mem_k, cache_mem_v,
        state_ssm_re.reshape(n_b, dec_batch, SSM_LANES),
        state_ssm_im.reshape(n_b, dec_batch, SSM_LANES), w)

    def states(parts, n):
        return jnp.stack(parts).reshape(n_b, n, N_GROUPS, STATE)

    return (y_prompt.reshape(batch, seq, D_MODEL),
            y_sample.reshape(dec_batch, 1, D_MODEL),
            mem_k, mem_v,
            states(p_re, batch), states(p_im, batch),
            states(s_re, dec_batch), states(s_im, dec_batch),
            jnp.stack(s_v).reshape(n_a, dec_batch, 1, D_A))
```

```python
import functools
import math

import jax
import jax.numpy as jnp
from jax import lax
from jax.experimental import pallas as pl
from jax.experimental.pallas import tpu as pltpu

F32 = jnp.float32
BF16 = jnp.bfloat16

D_MODEL = 1024
DEPTH = 4
CHUNK = 128
D_A = 2 * D_MODEL
A_HEADS = 8
A_HEAD_DIM = D_A // A_HEADS
GROUP = 16
N_GROUPS = D_MODEL // GROUP
STATE = 64
SSM_LANES = N_GROUPS * STATE
N_MEM = 256
X_HEADS = 4
X_HEAD_DIM = D_MODEL // X_HEADS
D_FF = 2816
EPS = 1e-6

V7X_SUBLANES = 8
V7X_LANES = 128
V7X_VMEM_LIMIT_BYTES = 56 * 1024 * 1024
V7X_FFN_VMEM_LIMIT_BYTES = 60 * 1024 * 1024

ROW_TILE = 512
FFN_ROW_TILE = 1024
FFN_FF_CHUNK = 768
FFN_STREAM_COLS = 128
SSM_TIME_TILE = 64
SSM_GROUP_BLOCK = 16
SSM_BLOCKS = N_GROUPS // SSM_GROUP_BLOCK
SCAN_LANES = 512
MEM_KV_SEQS = 2


def _params(*semantics):
    return pltpu.CompilerParams(dimension_semantics=semantics,
                                vmem_limit_bytes=V7X_VMEM_LIMIT_BYTES)


def _const_spec(shape):
    nd = len(shape)
    return pl.BlockSpec(shape, lambda *_: (0,) * nd, pipeline_mode=pl.Buffered(1))


def _layer_spec(shape, layer):
    nd = len(shape)
    return pl.BlockSpec((None,) + tuple(shape), lambda *_: (layer,) + (0,) * nd,
                        pipeline_mode=pl.Buffered(1))


def _rmsnorm(x, g):
    return x * lax.rsqrt(jnp.mean(x * x, axis=-1, keepdims=True) + EPS) * g


def _dot(a, b):
    return jnp.dot(a, b, preferred_element_type=F32)


def _ffn_kernel(x_ref, xs_ref, g_ref, wg_hbm, wu_hbm, wd_hbm, gf_ref, o_ref, os_ref,
                wg_ref, wu_ref, wd_ref, sg_ref, su_ref, sd_ref, sem, *, layer, final_norm):
    step = pl.program_id(0)
    pieces = [(c0, min(FFN_STREAM_COLS, D_FF - c0)) for c0 in range(0, D_FF, FFN_STREAM_COLS)]

    def copies(p):
        c0, w = pieces[p]
        slot = p % 2
        return (pltpu.make_async_copy(wg_hbm.at[layer, :, pl.ds(c0, w)],
                                      sg_ref.at[slot, :, pl.ds(0, w)], sem.at[0, slot]),
                pltpu.make_async_copy(wu_hbm.at[layer, :, pl.ds(c0, w)],
                                      su_ref.at[slot, :, pl.ds(0, w)], sem.at[1, slot]),
                pltpu.make_async_copy(wd_hbm.at[layer, pl.ds(c0, w), :],
                                      sd_ref.at[slot, pl.ds(0, w), :], sem.at[2, slot]))

    def fetch(p):
        for cp in copies(p):
            cp.start()

    def land(p):
        c0, w = pieces[p]
        slot = p % 2
        for cp in copies(p):
            cp.wait()
        wg_ref[:, c0:c0 + w] = sg_ref[slot, :, :w].astype(BF16)
        wu_ref[:, c0:c0 + w] = su_ref[slot, :, :w].astype(BF16)
        wd_ref[c0:c0 + w, :] = sd_ref[slot, :w, :].astype(BF16)
        if p + 2 < len(pieces):
            fetch(p + 2)

    def land_columns(c0, c1):
        for p, (p0, w) in enumerate(pieces):
            if c0 <= p0 and p0 + w <= c1:
                land(p)

    def ffn_rows(x, before_chunk=None):
        h = _rmsnorm(x, g_ref[...]).astype(BF16)
        down = None
        for c0 in range(0, D_FF, FFN_FF_CHUNK):
            c1 = min(c0 + FFN_FF_CHUNK, D_FF)
            if before_chunk is not None:
                before_chunk(c0, c1)
            cols = slice(c0, c1)
            gate = _dot(h, wg_ref[:, cols])
            up = _dot(h, wu_ref[:, cols])
            act = (jax.nn.silu(gate) * up).astype(BF16)
            part = _dot(act, wd_ref[cols, :])
            down = part if down is None else down + part
        y = x + 0.5 * down
        return _rmsnorm(y, gf_ref[...]) if final_norm else y

    @pl.when(step == 0)
    def _():
        fetch(0)
        fetch(1)
        o_ref[...] = ffn_rows(x_ref[...], land_columns)

    @pl.when(step != 0)
    def _():
        o_ref[...] = ffn_rows(x_ref[...])

    @pl.when(step == pl.num_programs(0) - 1)
    def _():
        os_ref[...] = ffn_rows(xs_ref[...])


def _ffn(x, xs, g, wg, wu, wd, g_final, layer, final_norm):
    rows = x.shape[0]
    tm = FFN_ROW_TILE
    assert FFN_FF_CHUNK % FFN_STREAM_COLS == 0 and rows // tm > 1
    row_spec = pl.BlockSpec((tm, D_MODEL), lambda i: (i, 0))
    hbm_spec = pl.BlockSpec(memory_space=pl.ANY)
    return pl.pallas_call(
        functools.partial(_ffn_kernel, layer=layer, final_norm=final_norm),
        out_shape=(jax.ShapeDtypeStruct(x.shape, F32), jax.ShapeDtypeStruct(xs.shape, F32)),
        grid=(rows // tm,),
        in_specs=[row_spec,
                  _const_spec(xs.shape),
                  _layer_spec((1, D_MODEL), layer),
                  hbm_spec, hbm_spec, hbm_spec,
                  _const_spec((1, D_MODEL))],
        out_specs=(row_spec, _const_spec(xs.shape)),
        scratch_shapes=[pltpu.VMEM((D_MODEL, D_FF), BF16), pltpu.VMEM((D_MODEL, D_FF), BF16),
                        pltpu.VMEM((D_FF, D_MODEL), BF16),
                        pltpu.VMEM((2, D_MODEL, FFN_STREAM_COLS), F32),
                        pltpu.VMEM((2, D_MODEL, FFN_STREAM_COLS), F32),
                        pltpu.VMEM((2, FFN_STREAM_COLS, D_MODEL), F32),
                        pltpu.SemaphoreType.DMA((3, 2))],
        compiler_params=pltpu.CompilerParams(dimension_semantics=("arbitrary",),
                                             vmem_limit_bytes=V7X_FFN_VMEM_LIMIT_BYTES),
        name="ffn",
    )(x, xs, g, wg, wu, wd, g_final)


def _gated_unit_inputs(x_ref, g_ref, w_in_ref, ln_g_ref, ln_b_ref):
    h = _rmsnorm(x_ref[...], g_ref[...]).astype(BF16)
    uv = jax.nn.gelu(_dot(h, w_in_ref[...]))
    u = uv[:, :D_A]
    v = uv[:, D_A:]
    mu = jnp.mean(v, axis=-1, keepdims=True)
    vc = v - mu
    var = jnp.mean(vc * vc, axis=-1, keepdims=True)
    v = vc * lax.rsqrt(var + EPS) * ln_g_ref[...] + ln_b_ref[...]
    return u, v


def _mixer_a_kernel(x_ref, xs_ref, g_ref, w_in_ref, ln_g_ref, ln_b_ref, ws_ref, bias_ref,
                    scale0_ref, bias0_ref, w_out_ref, o_ref, os_ref, vs_ref, gated_ref):
    u, v = _gated_unit_inputs(x_ref, g_ref, w_in_ref, ln_g_ref, ln_b_ref)
    vb = v.astype(BF16)
    t_idx = lax.broadcasted_iota(jnp.int32, (CHUNK, CHUNK), 0)
    s_idx = lax.broadcasted_iota(jnp.int32, (CHUNK, CHUNK), 1)
    causal = t_idx >= s_idx
    for head in range(A_HEADS):
        cols = slice(head * A_HEAD_DIM, (head + 1) * A_HEAD_DIM)
        ws = jnp.where(causal, ws_ref[head], 0.0).astype(BF16)
        bias = bias_ref[:, cols]
        for c in range(x_ref.shape[0] // CHUNK):
            rows = slice(c * CHUNK, (c + 1) * CHUNK)
            mixed = _dot(ws, vb[rows, cols]) + bias
            gated_ref[rows, cols] = (u[rows, cols] * mixed).astype(BF16)
    o_ref[...] = x_ref[...] + _dot(gated_ref[...], w_out_ref[...])

    @pl.when(pl.program_id(0) == pl.num_programs(0) - 1)
    def _():
        us, vs = _gated_unit_inputs(xs_ref, g_ref, w_in_ref, ln_g_ref, ln_b_ref)
        vs_ref[...] = vs
        mixed = vs * scale0_ref[...] + bias0_ref[...]
        os_ref[...] = xs_ref[...] + _dot((us * mixed).astype(BF16), w_out_ref[...])


def _mixer_a(x, xs, g, w_in, ln_g, ln_b, w_s, bias_full, scale0, bias0, w_out, layer, j):
    rows = x.shape[0]
    tm = ROW_TILE
    row_spec = pl.BlockSpec((tm, D_MODEL), lambda i: (i, 0))
    vs_shape = (xs.shape[0], D_A)
    return pl.pallas_call(
        _mixer_a_kernel,
        out_shape=(jax.ShapeDtypeStruct(x.shape, F32), jax.ShapeDtypeStruct(xs.shape, F32),
                   jax.ShapeDtypeStruct(vs_shape, F32)),
        grid=(rows // tm,),
        in_specs=[row_spec,
                  _const_spec(xs.shape),
                  _layer_spec((1, D_MODEL), layer),
                  _layer_spec((D_MODEL, 2 * D_A), j),
                  _layer_spec((1, D_A), j),
                  _layer_spec((1, D_A), j),
                  _layer_spec((A_HEADS, CHUNK, CHUNK), j),
                  _layer_spec((CHUNK, D_A), j),
                  _layer_spec((1, D_A), j),
                  _layer_spec((1, D_A), j),
                  _layer_spec((D_A, D_MODEL), j)],
        out_specs=(row_spec, _const_spec(xs.shape), _const_spec(vs_shape)),
        scratch_shapes=[pltpu.VMEM((tm, D_A), BF16)],
        compiler_params=_params("arbitrary"),
        name="mixer_a",
    )(x, xs, g, w_in, ln_g, ln_b, w_s, bias_full, scale0, bias0, w_out)


def _ssm_discretize_kernel(a_re_ref, a_im_ref, log_dt_ref, b_re_ref, b_im_ref,
                           ab_re_ref, ab_im_ref, bb_re_ref, bb_im_ref):
    a_re = a_re_ref[...]
    a_im = a_im_ref[...]
    dt = jnp.exp(log_dt_ref[...])
    mag = jnp.exp(a_re * dt)
    ab_re = mag * jnp.cos(a_im * dt)
    ab_im = mag * jnp.sin(a_im * dt)
    den = a_re * a_re + a_im * a_im
    nr = ab_re - 1.0
    ni = ab_im
    f_re = (nr * a_re + ni * a_im) / den
    f_im = (ni * a_re - nr * a_im) / den
    b_re = b_re_ref[...]
    b_im = b_im_ref[...]
    ab_re_ref[...] = ab_re
    ab_im_ref[...] = ab_im
    bb_re_ref[...] = f_re * b_re - f_im * b_im
    bb_im_ref[...] = f_re * b_im + f_im * b_re


def _ssm_discretize(a_re, a_im, log_dt, b_re_t, b_im_t):
    n_layers = a_re.shape[0]
    ab_shape = jax.ShapeDtypeStruct((n_layers, N_GROUPS, 1, STATE), F32)
    bb_shape = jax.ShapeDtypeStruct((n_layers, N_GROUPS, GROUP, STATE), F32)
    return pl.pallas_call(
        _ssm_discretize_kernel,
        out_shape=(ab_shape, ab_shape, bb_shape, bb_shape),
        name="ssm_discretize",
    )(a_re.reshape(ab_shape.shape), a_im.reshape(ab_shape.shape),
      log_dt.reshape(n_layers, N_GROUPS, 1, 1), b_re_t, b_im_t)


def _ssm_kernel(x_ref, g_ref, wb_ref, ab_re_ref, ab_im_ref, wc_re_ref, wc_im_ref, d_ref, wglu_ref,
                h0_re_ref, h0_im_ref, o_ref, hf_re_ref, hf_im_ref,
                s_re_ref, s_im_ref, st_re_ref, st_im_ref, tm_ref, abb_re_ref, abb_im_ref,
                *, n_batch, n_time):
    step = pl.program_id(0)
    blk = SSM_GROUP_BLOCK * STATE
    cblk = SSM_GROUP_BLOCK * GROUP
    lane_tiles = [slice(j * V7X_LANES, (j + 1) * V7X_LANES) for j in range(D_MODEL // V7X_LANES)]

    @pl.when(step == 0)
    def _():
        st_re_ref[...] = h0_re_ref[...]
        st_im_ref[...] = h0_im_ref[...]
        abb_re_ref[...] = jnp.broadcast_to(ab_re_ref[...], abb_re_ref.shape)
        abb_im_ref[...] = jnp.broadcast_to(ab_im_ref[...], abb_im_ref.shape)

    if n_time == 1:
        u = _rmsnorm(x_ref[...], g_ref[...])
    else:
        for b in range(n_batch):
            ub_rows = _rmsnorm(x_ref[b], g_ref[...])
            for j, cols in enumerate(lane_tiles):
                tm_ref[j, pl.ds(b, n_time, stride=n_batch), :] = ub_rows[:, cols]
        u = jnp.concatenate([tm_ref[j] for j in range(len(lane_tiles))], axis=1)
    ub = u.astype(BF16)

    def project_in(k):
        bu = _dot(ub[:, k * cblk:(k + 1) * cblk], wb_ref[k])
        s_re_ref[:, k * blk:(k + 1) * blk] = bu[:, :blk]
        s_im_ref[:, k * blk:(k + 1) * blk] = bu[:, blk:]

    def scan(c):
        lanes = slice(c * SCAN_LANES, (c + 1) * SCAN_LANES)
        ar = abb_re_ref[:, lanes]
        ai = abb_im_ref[:, lanes]
        for s in range(n_batch // V7X_SUBLANES):
            rows0 = slice(s * V7X_SUBLANES, (s + 1) * V7X_SUBLANES)
            hr = st_re_ref[rows0, lanes]
            hi = st_im_ref[rows0, lanes]
            for t in range(n_time):
                rows = slice(t * n_batch + s * V7X_SUBLANES, t * n_batch + (s + 1) * V7X_SUBLANES)
                hr, hi = (ar * hr - ai * hi + s_re_ref[rows, lanes],
                          ar * hi + ai * hr + s_im_ref[rows, lanes])
                s_re_ref[rows, lanes] = hr
                s_im_ref[rows, lanes] = hi
            st_re_ref[rows0, lanes] = hr
            st_im_ref[rows0, lanes] = hi

    def project_out(k):
        hr = s_re_ref[:, k * blk:(k + 1) * blk].astype(BF16)
        hi = s_im_ref[:, k * blk:(k + 1) * blk].astype(BF16)
        return _dot(hr, wc_re_ref[k]) - _dot(hi, wc_im_ref[k])

    ys = []
    if n_time == 1:
        for k in range(SSM_BLOCKS):
            project_in(k)
        ar = ab_re_ref[...]
        ai = ab_im_ref[...]
        hr0 = st_re_ref[...]
        hi0 = st_im_ref[...]
        hr = ar * hr0 - ai * hi0 + s_re_ref[...]
        hi = ar * hi0 + ai * hr0 + s_im_ref[...]
        s_re_ref[...] = hr
        s_im_ref[...] = hi
        st_re_ref[...] = hr
        st_im_ref[...] = hi
        for k in range(SSM_BLOCKS):
            ys.append(project_out(k))
    else:
        chunks_per_block = blk // SCAN_LANES
        project_in(0)
        for k in range(SSM_BLOCKS):
            if k + 1 < SSM_BLOCKS:
                project_in(k + 1)
            for c in range(chunks_per_block):
                scan(k * chunks_per_block + c)
            ys.append(project_out(k))
    y = jnp.concatenate(ys, axis=-1) + d_ref[...] * u
    z = jax.nn.gelu(y).astype(BF16)
    vg = _dot(z, wglu_ref[...])
    out = vg[:, :D_MODEL] * jax.nn.sigmoid(vg[:, D_MODEL:])
    if n_time == 1:
        o_ref[...] = x_ref[...] + out
    else:
        for j, cols in enumerate(lane_tiles):
            tm_ref[j] = out[:, cols]
        for b in range(n_batch):
            rows = pl.ds(b, n_time, stride=n_batch)
            out_b = jnp.concatenate([tm_ref[j, rows, :] for j in range(len(lane_tiles))], axis=1)
            o_ref[b] = x_ref[b] + out_b

    @pl.when(step == pl.num_programs(0) - 1)
    def _():
        hf_re_ref[...] = st_re_ref[...]
        hf_im_ref[...] = st_im_ref[...]


def _ssm_mixer(x, g, wb, ab_re, ab_im, wc_re, wc_im, d_skip, w_glu, h0_re, h0_im,
               layer, j, n_batch, n_time_total):
    n_time = min(SSM_TIME_TILE, n_time_total)
    tile = n_time * n_batch
    if n_time_total == 1:
        x_in = x
        row_spec = _const_spec((n_batch, D_MODEL))
    else:
        x_in = x.reshape(n_batch, n_time_total, D_MODEL)
        row_spec = pl.BlockSpec((n_batch, n_time, D_MODEL), lambda i: (0, i, 0))
    state_spec = _const_spec((n_batch, SSM_LANES))
    state_shape = jax.ShapeDtypeStruct((n_batch, SSM_LANES), F32)
    blk = SSM_GROUP_BLOCK * STATE
    cblk = SSM_GROUP_BLOCK * GROUP
    out, h_re, h_im = pl.pallas_call(
        functools.partial(_ssm_kernel, n_batch=n_batch, n_time=n_time),
        out_shape=(jax.ShapeDtypeStruct(x_in.shape, F32), state_shape, state_shape),
        grid=(n_time_total // n_time,),
        in_specs=[row_spec,
                  _layer_spec((1, D_MODEL), layer),
                  _layer_spec((SSM_BLOCKS, cblk, 2 * blk), j),
                  _layer_spec((1, SSM_LANES), j),
                  _layer_spec((1, SSM_LANES), j),
                  _layer_spec((SSM_BLOCKS, blk, cblk), j),
                  _layer_spec((SSM_BLOCKS, blk, cblk), j),
                  _layer_spec((1, D_MODEL), j),
                  _layer_spec((D_MODEL, 2 * D_MODEL), j),
                  state_spec, state_spec],
        out_specs=(row_spec, state_spec, state_spec),
        scratch_shapes=[pltpu.VMEM((tile, SSM_LANES), F32), pltpu.VMEM((tile, SSM_LANES), F32),
                        pltpu.VMEM((n_batch, SSM_LANES), F32), pltpu.VMEM((n_batch, SSM_LANES), F32),
                        pltpu.VMEM((D_MODEL // V7X_LANES, tile, V7X_LANES), F32),
                        pltpu.VMEM((V7X_SUBLANES, SSM_LANES), F32),
                        pltpu.VMEM((V7X_SUBLANES, SSM_LANES), F32)],
        compiler_params=_params("arbitrary"),
        name="ssm_mixer",
    )(x_in, g, wb, ab_re, ab_im, wc_re, wc_im, d_skip, w_glu, h0_re, h0_im)
    return out.reshape(x.shape), h_re, h_im


def _block_diag(w, rows_inner, cols_inner):
    n_layers = w.shape[0]
    rows, cols = SSM_GROUP_BLOCK * rows_inner, SSM_GROUP_BLOCK * cols_inner
    tiled = jnp.tile(w.reshape(n_layers, SSM_BLOCKS, rows, cols_inner), (1, 1, 1, SSM_GROUP_BLOCK))
    row_group = lax.broadcasted_iota(jnp.int32, (rows, cols), 0) // rows_inner
    col_group = lax.broadcasted_iota(jnp.int32, (rows, cols), 1) // cols_inner
    return jnp.where(row_group == col_group, tiled, jnp.zeros_like(tiled))


def _mem_kv_kernel(m_ref, g_ref, wk_ref, wv_ref, k_ref, v_ref, kb_ref, vb_ref):
    m = _rmsnorm(m_ref[...], g_ref[...]).astype(BF16)
    k = _dot(m, wk_ref[...])
    v = _dot(m, wv_ref[...])
    kb_ref[...] = k.astype(BF16)
    vb_ref[...] = v.astype(BF16)
    for s in range(MEM_KV_SEQS):
        rows = slice(s * N_MEM, (s + 1) * N_MEM)
        for h in range(X_HEADS):
            cols = slice(h * X_HEAD_DIM, (h + 1) * X_HEAD_DIM)
            k_ref[s, :, h, :] = k[rows, cols]
            v_ref[s, :, h, :] = v[rows, cols]


def _mem_kv(mem, g, wk, wv):
    n_seq = mem.shape[0]
    tm = MEM_KV_SEQS * N_MEM
    out = jax.ShapeDtypeStruct((DEPTH, n_seq, N_MEM, X_HEADS, X_HEAD_DIM), F32)
    out_b = jax.ShapeDtypeStruct((DEPTH, n_seq * N_MEM, D_MODEL), BF16)
    w_spec = pl.BlockSpec((None, D_MODEL, D_MODEL), lambda l, i: (l, 0, 0))
    o_spec = pl.BlockSpec((None, MEM_KV_SEQS, N_MEM, X_HEADS, X_HEAD_DIM), lambda l, i: (l, i, 0, 0, 0))
    ob_spec = pl.BlockSpec((None, tm, D_MODEL), lambda l, i: (l, i, 0))
    return pl.pallas_call(
        _mem_kv_kernel,
        out_shape=(out, out, out_b, out_b),
        grid=(DEPTH, n_seq // MEM_KV_SEQS),
        in_specs=[pl.BlockSpec((tm, D_MODEL), lambda l, i: (i, 0)),
                  pl.BlockSpec((None, 1, D_MODEL), lambda l, i: (l, 0, 0)),
                  w_spec, w_spec],
        out_specs=(o_spec, o_spec, ob_spec, ob_spec),
        compiler_params=_params("parallel", "parallel"),
        name="mem_kv",
    )(mem.reshape(n_seq * N_MEM, D_MODEL), g, wk, wv)


def _softmax_rows(s):
    e = jnp.exp(s - jnp.max(s, axis=-1, keepdims=True))
    return e / jnp.sum(e, axis=-1, keepdims=True)


def _attn_kernel(x_ref, xs_ref, g_ref, wq_ref, k_ref, v_ref, ck_ref, cv_ref, wo_ref, o_ref, os_ref,
                 oh_ref, qs_ref, ohs_ref, *, nb):
    b, t = pl.program_id(0), pl.program_id(1)
    step = b * pl.num_programs(1) + t
    last = pl.num_programs(0) * pl.num_programs(1) - 1
    scale = X_HEAD_DIM ** -0.5
    head_cols = [slice(h * X_HEAD_DIM, (h + 1) * X_HEAD_DIM) for h in range(X_HEADS)]

    @pl.when(step == 0)
    def _():
        hs = _rmsnorm(xs_ref[...], g_ref[...]).astype(BF16)
        qs_ref[...] = _dot(hs, wq_ref[...])
        ohs_ref[...] = jnp.zeros_like(ohs_ref)

    x = x_ref[...]
    h = _rmsnorm(x, g_ref[...]).astype(BF16)
    q = _dot(h, wq_ref[...])
    for cols in head_cols:
        s = lax.dot_general(q[:, cols].astype(BF16), k_ref[:, cols], (((1,), (1,)), ((), ())),
                            preferred_element_type=F32) * scale
        p = _softmax_rows(s).astype(BF16)
        oh_ref[:, cols] = _dot(p, v_ref[:, cols]).astype(BF16)
    o_ref[...] = x + _dot(oh_ref[...], wo_ref[...])

    parts = V7X_SUBLANES // nb
    sub = step % parts
    row0 = pl.multiple_of((step // parts) * V7X_SUBLANES, V7X_SUBLANES)
    q_group = qs_ref[pl.ds(row0, V7X_SUBLANES), :]
    q_rows = q_group[:nb, :]
    for c in range(1, parts):
        q_rows = jnp.where(sub == c, q_group[c * nb:(c + 1) * nb, :], q_rows)
    n_rows = N_MEM * X_HEADS
    head_of_lane = lax.broadcasted_iota(jnp.int32, (V7X_SUBLANES, n_rows), 1) % X_HEADS
    head_of_row = lax.broadcasted_iota(jnp.int32, (V7X_SUBLANES, n_rows), 0) % X_HEADS
    own_head = head_of_lane == head_of_row
    scores = []
    for r in range(nb):
        qr = q_rows[r:r + 1, :]
        qh = jnp.concatenate([qr[:, c] for c in head_cols] * (V7X_SUBLANES // X_HEADS), axis=0)
        k2 = ck_ref[r].reshape(n_rows, X_HEAD_DIM).astype(BF16)
        scores.append(lax.dot_general(qh.astype(BF16), k2, (((1,), (1,)), ((), ())),
                                      preferred_element_type=F32))
    sc = jnp.concatenate(scores, axis=0) * scale
    own = jnp.concatenate([own_head] * nb, axis=0)
    p_all = _softmax_rows(jnp.where(own, sc, -jnp.inf)).astype(BF16)
    outs = []
    for r in range(nb):
        v2 = cv_ref[r].reshape(n_rows, X_HEAD_DIM).astype(BF16)
        pv = _dot(p_all[r * V7X_SUBLANES:(r + 1) * V7X_SUBLANES, :], v2)
        outs.append(jnp.concatenate([pv[hd:hd + 1, :] for hd in range(X_HEADS)], axis=1))
    o_group = jnp.concatenate(outs * parts, axis=0)
    part_of_row = lax.broadcasted_iota(jnp.int32, (V7X_SUBLANES, D_MODEL), 0) // nb
    rows = pl.ds(row0, V7X_SUBLANES)
    ohs_ref[rows, :] = jnp.where(part_of_row == sub, o_group, ohs_ref[rows, :])

    @pl.when(step == last)
    def _():
        os_ref[...] = xs_ref[...] + _dot(ohs_ref[...].astype(BF16), wo_ref[...])


def _attn(x, xs, g, wq, k, v, cache_k, cache_v, wo, layer, n_seq, seq_len):
    tm = ROW_TILE
    n_t = seq_len // tm
    nb, rem = divmod(xs.shape[0], n_seq * n_t)
    assert rem == 0 and V7X_SUBLANES % nb == 0
    row_spec = pl.BlockSpec((tm, D_MODEL), lambda b, i: (b * n_t + i, 0))
    kv_spec = pl.BlockSpec((None, N_MEM, D_MODEL), lambda b, i: (layer, b, 0))
    cache_spec = pl.BlockSpec((None, nb, N_MEM, X_HEADS, X_HEAD_DIM),
                              lambda b, i: (layer, b * n_t + i, 0, 0, 0))
    return pl.pallas_call(
        functools.partial(_attn_kernel, nb=nb),
        out_shape=(jax.ShapeDtypeStruct(x.shape, F32), jax.ShapeDtypeStruct(xs.shape, F32)),
        grid=(n_seq, n_t),
        in_specs=[row_spec,
                  _const_spec(xs.shape),
                  _layer_spec((1, D_MODEL), layer),
                  _layer_spec((D_MODEL, D_MODEL), layer),
                  kv_spec, kv_spec, cache_spec, cache_spec,
                  _layer_spec((D_MODEL, D_MODEL), layer)],
        out_specs=(row_spec, _const_spec(xs.shape)),
        scratch_shapes=[pltpu.VMEM((tm, D_MODEL), BF16),
                        pltpu.VMEM(xs.shape, F32), pltpu.VMEM(xs.shape, F32)],
        compiler_params=_params("arbitrary", "arbitrary"),
        name="attn",
    )(x, xs, g, wq, k, v, cache_k, cache_v, wo)


def _trunk(x, xs, n_seq, seq_len, mem_k, mem_v, cache_k, cache_v, state_re, state_im, w):
    n_samples = xs.shape[0]
    zero_state = jnp.zeros((n_seq, SSM_LANES), F32)
    p_re, p_im, s_re, s_im, s_v = [], [], [], [], []
    for i in range(DEPTH):
        j = i // 2
        x, xs = _ffn(x, xs, w['norm_ffn1'], w['ffn1_wg'], w['ffn1_wu'], w['ffn1_wd'], w['norm_final'],
                     i, False)
        if i % 2 == 0:
            x, xs, v_rows = _mixer_a(x, xs, w['norm_mix'], w['a_w_in'], w['a_ln_g'], w['a_ln_b'],
                                     w['a_w_s'], w['a_bias_full'], w['a_scale0'], w['a_bias0'],
                                     w['a_w_out'], i, j)
            s_v.append(v_rows)
        else:
            ssm_w = (w['norm_mix'], w['b_wb'], w['b_ab_re'], w['b_ab_im'], w['b_wc_re'], w['b_wc_im'],
                     w['b_d'], w['b_w_glu'])
            x, hr, hi = _ssm_mixer(x, *ssm_w, zero_state, zero_state, i, j, n_seq, seq_len)
            p_re.append(hr)
            p_im.append(hi)
            xs, hr, hi = _ssm_mixer(xs, *ssm_w, state_re[j], state_im[j], i, j, n_samples, 1)
            s_re.append(hr)
            s_im.append(hi)
        x, xs = _attn(x, xs, w['norm_x'], w['x_wq'], mem_k, mem_v, cache_k, cache_v, w['x_wo'],
                      i, n_seq, seq_len)
        x, xs = _ffn(x, xs, w['norm_ffn2'], w['ffn2_wg'], w['ffn2_wu'], w['ffn2_wd'], w['norm_final'],
                     i, i == DEPTH - 1)
    return x, xs, p_re, p_im, s_re, s_im, s_v


def kernel(x_prompt, x_sample, cache_mem_k, cache_mem_v, state_ssm_re, state_ssm_im, mem_prompt,
           norm_ffn1, ffn1_wg, ffn1_wu, ffn1_wd, norm_mix,
           a_w_in, a_ln_g, a_ln_b, a_w_s, a_b_s, a_w_out,
           b_a_re, b_a_im, b_log_dt, b_b_re, b_b_im, b_c_re, b_c_im, b_d, b_w_glu,
           norm_x, norm_mem, x_wq, x_wk, x_wv, x_wo,
           norm_ffn2, ffn2_wg, ffn2_wu, ffn2_wd, norm_final):
    batch, seq, _ = x_prompt.shape
    dec_batch, dec_seq, _ = x_sample.shape
    assert dec_seq == 1 and seq % ROW_TILE == 0 and ROW_TILE % CHUNK == 0
    n_a = a_w_in.shape[0]
    n_b = b_a_re.shape[0]

    def vec(p):
        return p.reshape(p.shape[0], 1, p.shape[-1])

    ab_re, ab_im, bb_re, bb_im = _ssm_discretize(
        b_a_re, b_a_im, b_log_dt, b_b_re.transpose(0, 1, 3, 2), b_b_im.transpose(0, 1, 3, 2))
    wb = jnp.concatenate([_block_diag(bb_re.astype(BF16), GROUP, STATE),
                          _block_diag(bb_im.astype(BF16), GROUP, STATE)], axis=-1)
    w = dict(
        norm_ffn1=vec(norm_ffn1), norm_ffn2=vec(norm_ffn2), norm_mix=vec(norm_mix), norm_x=vec(norm_x),
        norm_final=norm_final.reshape(1, D_MODEL),
        ffn1_wg=ffn1_wg, ffn1_wu=ffn1_wu, ffn1_wd=ffn1_wd,
        ffn2_wg=ffn2_wg, ffn2_wu=ffn2_wu, ffn2_wd=ffn2_wd,
        a_w_in=a_w_in.astype(BF16), a_w_out=a_w_out.astype(BF16),
        a_ln_g=vec(a_ln_g), a_ln_b=vec(a_ln_b), a_w_s=a_w_s,
        a_bias_full=jnp.repeat(jnp.swapaxes(a_b_s, 1, 2), A_HEAD_DIM, axis=-1),
        a_scale0=jnp.repeat(a_w_s[:, :, 0, 0], A_HEAD_DIM, axis=-1).reshape(n_a, 1, D_A),
        a_bias0=jnp.repeat(a_b_s[:, :, 0], A_HEAD_DIM, axis=-1).reshape(n_a, 1, D_A),
        b_wb=wb,
        b_ab_re=ab_re.reshape(n_b, 1, SSM_LANES), b_ab_im=ab_im.reshape(n_b, 1, SSM_LANES),
        b_wc_re=_block_diag(b_c_re.transpose(0, 1, 3, 2).astype(BF16), STATE, GROUP),
        b_wc_im=_block_diag(b_c_im.transpose(0, 1, 3, 2).astype(BF16), STATE, GROUP),
        b_d=b_d.reshape(n_b, 1, D_MODEL), b_w_glu=b_w_glu.astype(BF16),
        x_wq=x_wq.astype(BF16), x_wo=x_wo.astype(BF16),
    )

    mem_k, mem_v, mem_kb, mem_vb = _mem_kv(mem_prompt, vec(norm_mem), x_wk.astype(BF16), x_wv.astype(BF16))
    y_prompt, y_sample, p_re, p_im, s_re, s_im, s_v = _trunk(
        x_prompt.reshape(batch * seq, D_MODEL), x_sample.reshape(dec_batch, D_MODEL), batch, seq,
        mem_kb, mem_vb, cache_mem_k, cache_mem_v,
        state_ssm_re.reshape(n_b, dec_batch, SSM_LANES),
        state_ssm_im.reshape(n_b, dec_batch, SSM_LANES), w)

    def states(parts, n):
        return jnp.stack(parts).reshape(n_b, n, N_GROUPS, STATE)

    return (y_prompt.reshape(batch, seq, D_MODEL),
            y_sample.reshape(dec_batch, 1, D_MODEL),
            mem_k, mem_v,
            states(p_re, batch), states(p_im, batch),
            states(s_re, dec_batch), states(s_im, dec_batch),
            jnp.stack(s_v).reshape(n_a, dec_batch, 1, D_A))
```

```python
import functools
import math

import jax
import jax.numpy as jnp
from jax import lax
from jax.experimental import pallas as pl
from jax.experimental.pallas import tpu as pltpu

F32 = jnp.float32
BF16 = jnp.bfloat16

D_MODEL = 1024
DEPTH = 4
CHUNK = 128
D_A = 2 * D_MODEL
A_HEADS = 8
A_HEAD_DIM = D_A // A_HEADS
GROUP = 16
N_GROUPS = D_MODEL // GROUP
STATE = 64
SSM_LANES = N_GROUPS * STATE
N_MEM = 256
X_HEADS = 4
X_HEAD_DIM = D_MODEL // X_HEADS
D_FF = 2816
EPS = 1e-6

V7X_SUBLANES = 8
V7X_LANES = 128
V7X_VMEM_LIMIT_BYTES = 56 * 1024 * 1024

ROW_TILE = 512
FFN_ROW_TILE = 512
FFN_FF_CHUNK = 768
SSM_TIME_TILE = 64
SSM_GROUP_BLOCK = 16
SSM_BLOCKS = N_GROUPS // SSM_GROUP_BLOCK
SCAN_LANES = 512
MEM_KV_SEQS = 2


def _params(*semantics):
    return pltpu.CompilerParams(dimension_semantics=semantics,
                                vmem_limit_bytes=V7X_VMEM_LIMIT_BYTES)


def _const_spec(shape):
    nd = len(shape)
    return pl.BlockSpec(shape, lambda *_: (0,) * nd, pipeline_mode=pl.Buffered(1))


def _layer_spec(shape, layer):
    nd = len(shape)
    return pl.BlockSpec((None,) + tuple(shape), lambda *_: (layer,) + (0,) * nd,
                        pipeline_mode=pl.Buffered(1))


def _rmsnorm(x, g):
    return x * lax.rsqrt(jnp.mean(x * x, axis=-1, keepdims=True) + EPS) * g


def _dot(a, b):
    return lax.dot_general(a, b, (((1,), (0,)), ((), ())), preferred_element_type=F32)


def _ffn_kernel(x_ref, xs_ref, g_ref, wg_ref, wu_ref, wd_ref, gf_ref, o_ref, os_ref, *, final_norm):
    def ffn_rows(x):
        h = _rmsnorm(x, g_ref[...]).astype(BF16)
        down = None
        for c0 in range(0, D_FF, FFN_FF_CHUNK):
            cols = slice(c0, min(c0 + FFN_FF_CHUNK, D_FF))
            gate = _dot(h, wg_ref[:, cols])
            up = _dot(h, wu_ref[:, cols])
            act = (jax.nn.silu(gate) * up).astype(BF16)
            part = _dot(act, wd_ref[cols, :])
            down = part if down is None else down + part
        y = x + 0.5 * down
        return _rmsnorm(y, gf_ref[...]) if final_norm else y

    o_ref[...] = ffn_rows(x_ref[...])

    @pl.when(pl.program_id(0) == pl.num_programs(0) - 1)
    def _():
        os_ref[...] = ffn_rows(xs_ref[...])


def _ffn(x, xs, g, wg, wu, wd, g_final, layer, final_norm):
    rows = x.shape[0]
    tm = FFN_ROW_TILE
    row_spec = pl.BlockSpec((tm, D_MODEL), lambda i: (i, 0))
    return pl.pallas_call(
        functools.partial(_ffn_kernel, final_norm=final_norm),
        out_shape=(jax.ShapeDtypeStruct(x.shape, F32), jax.ShapeDtypeStruct(xs.shape, F32)),
        grid=(rows // tm,),
        in_specs=[row_spec,
                  _const_spec(xs.shape),
                  _layer_spec((1, D_MODEL), layer),
                  _layer_spec((D_MODEL, D_FF), layer),
                  _layer_spec((D_MODEL, D_FF), layer),
                  _layer_spec((D_FF, D_MODEL), layer),
                  _const_spec((1, D_MODEL))],
        out_specs=(row_spec, _const_spec(xs.shape)),
        compiler_params=_params("arbitrary"),
        name="ffn",
    )(x, xs, g, wg, wu, wd, g_final)


def _gated_unit_inputs(x_ref, g_ref, w_in_ref, ln_g_ref, ln_b_ref):
    h = _rmsnorm(x_ref[...], g_ref[...]).astype(BF16)
    uv = jax.nn.gelu(_dot(h, w_in_ref[...]))
    u = uv[:, :D_A]
    v = uv[:, D_A:]
    mu = jnp.mean(v, axis=-1, keepdims=True)
    vc = v - mu
    var = jnp.mean(vc * vc, axis=-1, keepdims=True)
    v = vc * lax.rsqrt(var + EPS) * ln_g_ref[...] + ln_b_ref[...]
    return u, v


def _mixer_a_kernel(x_ref, xs_ref, g_ref, w_in_ref, ln_g_ref, ln_b_ref, ws_ref, bias_ref,
                    scale0_ref, bias0_ref, w_out_ref, o_ref, os_ref, vs_ref, gated_ref):
    u, v = _gated_unit_inputs(x_ref, g_ref, w_in_ref, ln_g_ref, ln_b_ref)
    vb = v.astype(BF16)
    t_idx = lax.broadcasted_iota(jnp.int32, (CHUNK, CHUNK), 0)
    s_idx = lax.broadcasted_iota(jnp.int32, (CHUNK, CHUNK), 1)
    causal = t_idx >= s_idx
    for head in range(A_HEADS):
        cols = slice(head * A_HEAD_DIM, (head + 1) * A_HEAD_DIM)
        ws = jnp.where(causal, ws_ref[head], 0.0).astype(BF16)
        bias = bias_ref[:, cols]
        for c in range(x_ref.shape[0] // CHUNK):
            rows = slice(c * CHUNK, (c + 1) * CHUNK)
            mixed = _dot(ws, vb[rows, cols]) + bias
            gated_ref[rows, cols] = (u[rows, cols] * mixed).astype(BF16)
    o_ref[...] = x_ref[...] + _dot(gated_ref[...], w_out_ref[...])

    @pl.when(pl.program_id(0) == pl.num_programs(0) - 1)
    def _():
        us, vs = _gated_unit_inputs(xs_ref, g_ref, w_in_ref, ln_g_ref, ln_b_ref)
        vs_ref[...] = vs
        mixed = vs * scale0_ref[...] + bias0_ref[...]
        os_ref[...] = xs_ref[...] + _dot((us * mixed).astype(BF16), w_out_ref[...])


def _mixer_a(x, xs, g, w_in, ln_g, ln_b, w_s, bias_full, scale0, bias0, w_out, layer, j):
    rows = x.shape[0]
    tm = ROW_TILE
    row_spec = pl.BlockSpec((tm, D_MODEL), lambda i: (i, 0))
    vs_shape = (xs.shape[0], D_A)
    return pl.pallas_call(
        _mixer_a_kernel,
        out_shape=(jax.ShapeDtypeStruct(x.shape, F32), jax.ShapeDtypeStruct(xs.shape, F32),
                   jax.ShapeDtypeStruct(vs_shape, F32)),
        grid=(rows // tm,),
        in_specs=[row_spec,
                  _const_spec(xs.shape),
                  _layer_spec((1, D_MODEL), layer),
                  _layer_spec((D_MODEL, 2 * D_A), j),
                  _layer_spec((1, D_A), j),
                  _layer_spec((1, D_A), j),
                  _layer_spec((A_HEADS, CHUNK, CHUNK), j),
                  _layer_spec((CHUNK, D_A), j),
                  _layer_spec((1, D_A), j),
                  _layer_spec((1, D_A), j),
                  _layer_spec((D_A, D_MODEL), j)],
        out_specs=(row_spec, _const_spec(xs.shape), _const_spec(vs_shape)),
        scratch_shapes=[pltpu.VMEM((tm, D_A), BF16)],
        compiler_params=_params("arbitrary"),
        name="mixer_a",
    )(x, xs, g, w_in, ln_g, ln_b, w_s, bias_full, scale0, bias0, w_out)


def _ssm_discretize_kernel(a_re_ref, a_im_ref, log_dt_ref, b_re_ref, b_im_ref,
                           ab_re_ref, ab_im_ref, bb_re_ref, bb_im_ref):
    a_re = a_re_ref[...]
    a_im = a_im_ref[...]
    dt = jnp.exp(log_dt_ref[...])
    mag = jnp.exp(a_re * dt)
    ab_re = mag * jnp.cos(a_im * dt)
    ab_im = mag * jnp.sin(a_im * dt)
    den = a_re * a_re + a_im * a_im
    nr = ab_re - 1.0
    ni = ab_im
    f_re = (nr * a_re + ni * a_im) / den
    f_im = (ni * a_re - nr * a_im) / den
    b_re = b_re_ref[...]
    b_im = b_im_ref[...]
    ab_re_ref[...] = ab_re
    ab_im_ref[...] = ab_im
    bb_re_ref[...] = f_re * b_re - f_im * b_im
    bb_im_ref[...] = f_re * b_im + f_im * b_re


def _ssm_discretize(a_re, a_im, log_dt, b_re_t, b_im_t):
    n_layers = a_re.shape[0]
    ab_shape = jax.ShapeDtypeStruct((n_layers, N_GROUPS, 1, STATE), F32)
    bb_shape = jax.ShapeDtypeStruct((n_layers, N_GROUPS, GROUP, STATE), F32)
    return pl.pallas_call(
        _ssm_discretize_kernel,
        out_shape=(ab_shape, ab_shape, bb_shape, bb_shape),
        name="ssm_discretize",
    )(a_re.reshape(ab_shape.shape), a_im.reshape(ab_shape.shape),
      log_dt.reshape(n_layers, N_GROUPS, 1, 1), b_re_t, b_im_t)


def _ssm_kernel(x_ref, g_ref, wb_ref, ab_re_ref, ab_im_ref, wc_re_ref, wc_im_ref, d_ref, wglu_ref,
                h0_re_ref, h0_im_ref, o_ref, hf_re_ref, hf_im_ref,
                s_re_ref, s_im_ref, st_re_ref, st_im_ref, tm_ref, abb_re_ref, abb_im_ref,
                *, n_batch, n_time):
    step = pl.program_id(0)
    blk = SSM_GROUP_BLOCK * STATE
    cblk = SSM_GROUP_BLOCK * GROUP
    lane_tiles = [slice(j * V7X_LANES, (j + 1) * V7X_LANES) for j in range(D_MODEL // V7X_LANES)]

    @pl.when(step == 0)
    def _():
        st_re_ref[...] = h0_re_ref[...]
        st_im_ref[...] = h0_im_ref[...]
        abb_re_ref[...] = jnp.broadcast_to(ab_re_ref[...], abb_re_ref.shape)
        abb_im_ref[...] = jnp.broadcast_to(ab_im_ref[...], abb_im_ref.shape)

    if n_time == 1:
        u = _rmsnorm(x_ref[...], g_ref[...])
    else:
        for b in range(n_batch):
            ub_rows = _rmsnorm(x_ref[b], g_ref[...])
            for j, cols in enumerate(lane_tiles):
                tm_ref[j, pl.ds(b, n_time, stride=n_batch), :] = ub_rows[:, cols]
        u = jnp.concatenate([tm_ref[j] for j in range(len(lane_tiles))], axis=1)
    ub = u.astype(BF16)

    def project_in(k):
        bu = _dot(ub[:, k * cblk:(k + 1) * cblk], wb_ref[k])
        s_re_ref[:, k * blk:(k + 1) * blk] = bu[:, :blk]
        s_im_ref[:, k * blk:(k + 1) * blk] = bu[:, blk:]

    def scan(c):
        lanes = slice(c * SCAN_LANES, (c + 1) * SCAN_LANES)
        ar = abb_re_ref[:, lanes]
        ai = abb_im_ref[:, lanes]
        for s in range(n_batch // V7X_SUBLANES):
            rows0 = slice(s * V7X_SUBLANES, (s + 1) * V7X_SUBLANES)
            hr = st_re_ref[rows0, lanes]
            hi = st_im_ref[rows0, lanes]
            for t in range(n_time):
                rows = slice(t * n_batch + s * V7X_SUBLANES, t * n_batch + (s + 1) * V7X_SUBLANES)
                hr, hi = (ar * hr - ai * hi + s_re_ref[rows, lanes],
                          ar * hi + ai * hr + s_im_ref[rows, lanes])
                s_re_ref[rows, lanes] = hr
                s_im_ref[rows, lanes] = hi
            st_re_ref[rows0, lanes] = hr
            st_im_ref[rows0, lanes] = hi

    def project_out(k):
        hr = s_re_ref[:, k * blk:(k + 1) * blk].astype(BF16)
        hi = s_im_ref[:, k * blk:(k + 1) * blk].astype(BF16)
        return _dot(hr, wc_re_ref[k]) - _dot(hi, wc_im_ref[k])

    ys = []
    if n_time == 1:
        for k in range(SSM_BLOCKS):
            project_in(k)
        ar = ab_re_ref[...]
        ai = ab_im_ref[...]
        hr0 = st_re_ref[...]
        hi0 = st_im_ref[...]
        hr = ar * hr0 - ai * hi0 + s_re_ref[...]
        hi = ar * hi0 + ai * hr0 + s_im_ref[...]
        s_re_ref[...] = hr
        s_im_ref[...] = hi
        st_re_ref[...] = hr
        st_im_ref[...] = hi
        for k in range(SSM_BLOCKS):
            ys.append(project_out(k))
    else:
        chunks_per_block = blk // SCAN_LANES
        project_in(0)
        for k in range(SSM_BLOCKS):
            if k + 1 < SSM_BLOCKS:
                project_in(k + 1)
            for c in range(chunks_per_block):
                scan(k * chunks_per_block + c)
            ys.append(project_out(k))
    y = jnp.concatenate(ys, axis=-1) + d_ref[...] * u
    z = jax.nn.gelu(y).astype(BF16)
    vg = _dot(z, wglu_ref[...])
    out = vg[:, :D_MODEL] * jax.nn.sigmoid(vg[:, D_MODEL:])
    if n_time == 1:
        o_ref[...] = x_ref[...] + out
    else:
        for j, cols in enumerate(lane_tiles):
            tm_ref[j] = out[:, cols]
        for b in range(n_batch):
            rows = pl.ds(b, n_time, stride=n_batch)
            out_b = jnp.concatenate([tm_ref[j, rows, :] for j in range(len(lane_tiles))], axis=1)
            o_ref[b] = x_ref[b] + out_b

    @pl.when(step == pl.num_programs(0) - 1)
    def _():
        hf_re_ref[...] = st_re_ref[...]
        hf_im_ref[...] = st_im_ref[...]


def _ssm_mixer(x, g, wb, ab_re, ab_im, wc_re, wc_im, d_skip, w_glu, h0_re, h0_im,
               layer, j, n_batch, n_time_total):
    n_time = min(SSM_TIME_TILE, n_time_total)
    tile = n_time * n_batch
    if n_time_total == 1:
        x_in = x
        row_spec = _const_spec((n_batch, D_MODEL))
    else:
        x_in = x.reshape(n_batch, n_time_total, D_MODEL)
        row_spec = pl.BlockSpec((n_batch, n_time, D_MODEL), lambda i: (0, i, 0))
    state_spec = _const_spec((n_batch, SSM_LANES))
    state_shape = jax.ShapeDtypeStruct((n_batch, SSM_LANES), F32)
    blk = SSM_GROUP_BLOCK * STATE
    cblk = SSM_GROUP_BLOCK * GROUP
    out, h_re, h_im = pl.pallas_call(
        functools.partial(_ssm_kernel, n_batch=n_batch, n_time=n_time),
        out_shape=(jax.ShapeDtypeStruct(x_in.shape, F32), state_shape, state_shape),
        grid=(n_time_total // n_time,),
        in_specs=[row_spec,
                  _layer_spec((1, D_MODEL), layer),
                  _layer_spec((SSM_BLOCKS, cblk, 2 * blk), j),
                  _layer_spec((1, SSM_LANES), j),
                  _layer_spec((1, SSM_LANES), j),
                  _layer_spec((SSM_BLOCKS, blk, cblk), j),
                  _layer_spec((SSM_BLOCKS, blk, cblk), j),
                  _layer_spec((1, D_MODEL), j),
                  _layer_spec((D_MODEL, 2 * D_MODEL), j),
                  state_spec, state_spec],
        out_specs=(row_spec, state_spec, state_spec),
        scratch_shapes=[pltpu.VMEM((tile, SSM_LANES), F32), pltpu.VMEM((tile, SSM_LANES), F32),
                        pltpu.VMEM((n_batch, SSM_LANES), F32), pltpu.VMEM((n_batch, SSM_LANES), F32),
                        pltpu.VMEM((D_MODEL // V7X_LANES, tile, V7X_LANES), F32),
                        pltpu.VMEM((V7X_SUBLANES, SSM_LANES), F32),
                        pltpu.VMEM((V7X_SUBLANES, SSM_LANES), F32)],
        compiler_params=_params("arbitrary"),
        name="ssm_mixer",
    )(x_in, g, wb, ab_re, ab_im, wc_re, wc_im, d_skip, w_glu, h0_re, h0_im)
    return out.reshape(x.shape), h_re, h_im


def _block_diag(w, rows_inner, cols_inner):
    n_layers = w.shape[0]
    rows, cols = SSM_GROUP_BLOCK * rows_inner, SSM_GROUP_BLOCK * cols_inner
    tiled = jnp.tile(w.reshape(n_layers, SSM_BLOCKS, rows, cols_inner), (1, 1, 1, SSM_GROUP_BLOCK))
    row_group = lax.broadcasted_iota(jnp.int32, (rows, cols), 0) // rows_inner
    col_group = lax.broadcasted_iota(jnp.int32, (rows, cols), 1) // cols_inner
    return jnp.where(row_group == col_group, tiled, jnp.zeros_like(tiled))


def _mem_kv_kernel(m_ref, g_ref, wk_ref, wv_ref, k_ref, v_ref, kb_ref, vb_ref):
    m = _rmsnorm(m_ref[...], g_ref[...]).astype(BF16)
    k = _dot(m, wk_ref[...])
    v = _dot(m, wv_ref[...])
    kb_ref[...] = k.astype(BF16)
    vb_ref[...] = v.astype(BF16)
    for s in range(MEM_KV_SEQS):
        rows = slice(s * N_MEM, (s + 1) * N_MEM)
        for h in range(X_HEADS):
            cols = slice(h * X_HEAD_DIM, (h + 1) * X_HEAD_DIM)
            k_ref[s, :, h, :] = k[rows, cols]
            v_ref[s, :, h, :] = v[rows, cols]


def _mem_kv(mem, g, wk, wv):
    n_seq = mem.shape[0]
    tm = MEM_KV_SEQS * N_MEM
    out = jax.ShapeDtypeStruct((DEPTH, n_seq, N_MEM, X_HEADS, X_HEAD_DIM), F32)
    out_b = jax.ShapeDtypeStruct((DEPTH, n_seq * N_MEM, D_MODEL), BF16)
    w_spec = pl.BlockSpec((None, D_MODEL, D_MODEL), lambda l, i: (l, 0, 0))
    o_spec = pl.BlockSpec((None, MEM_KV_SEQS, N_MEM, X_HEADS, X_HEAD_DIM), lambda l, i: (l, i, 0, 0, 0))
    ob_spec = pl.BlockSpec((None, tm, D_MODEL), lambda l, i: (l, i, 0))
    return pl.pallas_call(
        _mem_kv_kernel,
        out_shape=(out, out, out_b, out_b),
        grid=(DEPTH, n_seq // MEM_KV_SEQS),
        in_specs=[pl.BlockSpec((tm, D_MODEL), lambda l, i: (i, 0)),
                  pl.BlockSpec((None, 1, D_MODEL), lambda l, i: (l, 0, 0)),
                  w_spec, w_spec],
        out_specs=(o_spec, o_spec, ob_spec, ob_spec),
        compiler_params=_params("parallel", "parallel"),
        name="mem_kv",
    )(mem.reshape(n_seq * N_MEM, D_MODEL), g, wk, wv)


def _softmax_rows(s):
    e = jnp.exp(s - jnp.max(s, axis=-1, keepdims=True))
    return e / jnp.sum(e, axis=-1, keepdims=True)


def _attn_kernel(x_ref, xs_ref, g_ref, wq_ref, k_ref, v_ref, ck_ref, cv_ref, wo_ref, o_ref, os_ref,
                 oh_ref, qs_ref, ohs_ref, *, nb):
    b, t = pl.program_id(0), pl.program_id(1)
    step = b * pl.num_programs(1) + t
    last = pl.num_programs(0) * pl.num_programs(1) - 1
    scale = X_HEAD_DIM ** -0.5
    head_cols = [slice(h * X_HEAD_DIM, (h + 1) * X_HEAD_DIM) for h in range(X_HEADS)]

    @pl.when(step == 0)
    def _():
        hs = _rmsnorm(xs_ref[...], g_ref[...]).astype(BF16)
        qs_ref[...] = _dot(hs, wq_ref[...])
        ohs_ref[...] = jnp.zeros_like(ohs_ref)

    x = x_ref[...]
    h = _rmsnorm(x, g_ref[...]).astype(BF16)
    q = _dot(h, wq_ref[...])
    for cols in head_cols:
        s = lax.dot_general(q[:, cols].astype(BF16), k_ref[:, cols], (((1,), (1,)), ((), ())),
                            preferred_element_type=F32) * scale
        p = _softmax_rows(s).astype(BF16)
        oh_ref[:, cols] = _dot(p, v_ref[:, cols]).astype(BF16)
    o_ref[...] = x + _dot(oh_ref[...], wo_ref[...])

    parts = V7X_SUBLANES // nb
    sub = step % parts
    row0 = pl.multiple_of((step // parts) * V7X_SUBLANES, V7X_SUBLANES)
    q_group = qs_ref[pl.ds(row0, V7X_SUBLANES), :]
    q_rows = q_group[:nb, :]
    for c in range(1, parts):
        q_rows = jnp.where(sub == c, q_group[c * nb:(c + 1) * nb, :], q_rows)
    n_rows = N_MEM * X_HEADS
    head_of_lane = lax.broadcasted_iota(jnp.int32, (V7X_SUBLANES, n_rows), 1) % X_HEADS
    head_of_row = lax.broadcasted_iota(jnp.int32, (V7X_SUBLANES, n_rows), 0) % X_HEADS
    own_head = head_of_lane == head_of_row
    scores = []
    for r in range(nb):
        qr = q_rows[r:r + 1, :]
        qh = jnp.concatenate([qr[:, c] for c in head_cols] * (V7X_SUBLANES // X_HEADS), axis=0)
        k2 = ck_ref[r].reshape(n_rows, X_HEAD_DIM).astype(BF16)
        scores.append(lax.dot_general(qh.astype(BF16), k2, (((1,), (1,)), ((), ())),
                                      preferred_element_type=F32))
    sc = jnp.concatenate(scores, axis=0) * scale
    own = jnp.concatenate([own_head] * nb, axis=0)
    p_all = _softmax_rows(jnp.where(own, sc, -jnp.inf)).astype(BF16)
    outs = []
    for r in range(nb):
        v2 = cv_ref[r].reshape(n_rows, X_HEAD_DIM).astype(BF16)
        pv = _dot(p_all[r * V7X_SUBLANES:(r + 1) * V7X_SUBLANES, :], v2)
        outs.append(jnp.concatenate([pv[hd:hd + 1, :] for hd in range(X_HEADS)], axis=1))
    o_group = jnp.concatenate(outs * parts, axis=0)
    part_of_row = lax.broadcasted_iota(jnp.int32, (V7X_SUBLANES, D_MODEL), 0) // nb
    rows = pl.ds(row0, V7X_SUBLANES)
    ohs_ref[rows, :] = jnp.where(part_of_row == sub, o_group, ohs_ref[rows, :])

    @pl.when(step == last)
    def _():
        os_ref[...] = xs_ref[...] + _dot(ohs_ref[...].astype(BF16), wo_ref[...])


def _attn(x, xs, g, wq, k, v, cache_k, cache_v, wo, layer, n_seq, seq_len):
    tm = ROW_TILE
    n_t = seq_len // tm
    nb, rem = divmod(xs.shape[0], n_seq * n_t)
    assert rem == 0 and V7X_SUBLANES % nb == 0
    row_spec = pl.BlockSpec((tm, D_MODEL), lambda b, i: (b * n_t + i, 0))
    kv_spec = pl.BlockSpec((None, N_MEM, D_MODEL), lambda b, i: (layer, b, 0))
    cache_spec = pl.BlockSpec((None, nb, N_MEM, X_HEADS, X_HEAD_DIM),
                              lambda b, i: (layer, b * n_t + i, 0, 0, 0))
    return pl.pallas_call(
        functools.partial(_attn_kernel, nb=nb),
        out_shape=(jax.ShapeDtypeStruct(x.shape, F32), jax.ShapeDtypeStruct(xs.shape, F32)),
        grid=(n_seq, n_t),
        in_specs=[row_spec,
                  _const_spec(xs.shape),
                  _layer_spec((1, D_MODEL), layer),
                  _layer_spec((D_MODEL, D_MODEL), layer),
                  kv_spec, kv_spec, cache_spec, cache_spec,
                  _layer_spec((D_MODEL, D_MODEL), layer)],
        out_specs=(row_spec, _const_spec(xs.shape)),
        scratch_shapes=[pltpu.VMEM((tm, D_MODEL), BF16),
                        pltpu.VMEM(xs.shape, F32), pltpu.VMEM(xs.shape, F32)],
        compiler_params=_params("arbitrary", "arbitrary"),
        name="attn",
    )(x, xs, g, wq, k, v, cache_k, cache_v, wo)


def _trunk(x, xs, n_seq, seq_len, mem_k, mem_v, cache_k, cache_v, state_re, state_im, w):
    n_samples = xs.shape[0]
    zero_state = jnp.zeros((n_seq, SSM_LANES), F32)
    p_re, p_im, s_re, s_im, s_v = [], [], [], [], []
    for i in range(DEPTH):
        j = i // 2
        x, xs = _ffn(x, xs, w['norm_ffn1'], w['ffn1_wg'], w['ffn1_wu'], w['ffn1_wd'], w['norm_final'],
                     i, False)
        if i % 2 == 0:
            x, xs, v_rows = _mixer_a(x, xs, w['norm_mix'], w['a_w_in'], w['a_ln_g'], w['a_ln_b'],
                                     w['a_w_s'], w['a_bias_full'], w['a_scale0'], w['a_bias0'],
                                     w['a_w_out'], i, j)
            s_v.append(v_rows)
        else:
            ssm_w = (w['norm_mix'], w['b_wb'], w['b_ab_re'], w['b_ab_im'], w['b_wc_re'], w['b_wc_im'],
                     w['b_d'], w['b_w_glu'])
            x, hr, hi = _ssm_mixer(x, *ssm_w, zero_state, zero_state, i, j, n_seq, seq_len)
            p_re.append(hr)
            p_im.append(hi)
            xs, hr, hi = _ssm_mixer(xs, *ssm_w, state_re[j], state_im[j], i, j, n_samples, 1)
            s_re.append(hr)
            s_im.append(hi)
        x, xs = _attn(x, xs, w['norm_x'], w['x_wq'], mem_k, mem_v, cache_k, cache_v, w['x_wo'],
                      i, n_seq, seq_len)
        x, xs = _ffn(x, xs, w['norm_ffn2'], w['ffn2_wg'], w['ffn2_wu'], w['ffn2_wd'], w['norm_final'],
                     i, i == DEPTH - 1)
    return x, xs, p_re, p_im, s_re, s_im, s_v


def kernel(x_prompt, x_sample, cache_mem_k, cache_mem_v, state_ssm_re, state_ssm_im, mem_prompt,
           norm_ffn1, ffn1_wg, ffn1_wu, ffn1_wd, norm_mix,
           a_w_in, a_ln_g, a_ln_b, a_w_s, a_b_s, a_w_out,
           b_a_re, b_a_im, b_log_dt, b_b_re, b_b_im, b_c_re, b_c_im, b_d, b_w_glu,
           norm_x, norm_mem, x_wq, x_wk, x_wv, x_wo,
           norm_ffn2, ffn2_wg, ffn2_wu, ffn2_wd, norm_final):
    batch, seq, _ = x_prompt.shape
    dec_batch, dec_seq, _ = x_sample.shape
    assert dec_seq == 1 and seq % ROW_TILE == 0 and ROW_TILE % CHUNK == 0
    n_a = a_w_in.shape[0]
    n_b = b_a_re.shape[0]

    def vec(p):
        return p.reshape(p.shape[0], 1, p.shape[-1])

    ab_re, ab_im, bb_re, bb_im = _ssm_discretize(
        b_a_re, b_a_im, b_log_dt, b_b_re.transpose(0, 1, 3, 2), b_b_im.transpose(0, 1, 3, 2))
    wb = jnp.concatenate([_block_diag(bb_re.astype(BF16), GROUP, STATE),
                          _block_diag(bb_im.astype(BF16), GROUP, STATE)], axis=-1)
    w = dict(
        norm_ffn1=vec(norm_ffn1), norm_ffn2=vec(norm_ffn2), norm_mix=vec(norm_mix), norm_x=vec(norm_x),
        norm_final=norm_final.reshape(1, D_MODEL),
        ffn1_wg=ffn1_wg, ffn1_wu=ffn1_wu, ffn1_wd=ffn1_wd,
        ffn2_wg=ffn2_wg, ffn2_wu=ffn2_wu, ffn2_wd=ffn2_wd,
        a_w_in=a_w_in, a_w_out=a_w_out,
        a_ln_g=vec(a_ln_g), a_ln_b=vec(a_ln_b), a_w_s=a_w_s,
        a_bias_full=jnp.repeat(jnp.swapaxes(a_b_s, 1, 2), A_HEAD_DIM, axis=-1),
        a_scale0=jnp.repeat(a_w_s[:, :, 0, 0], A_HEAD_DIM, axis=-1).reshape(n_a, 1, D_A),
        a_bias0=jnp.repeat(a_b_s[:, :, 0], A_HEAD_DIM, axis=-1).reshape(n_a, 1, D_A),
        b_wb=wb,
        b_ab_re=ab_re.reshape(n_b, 1, SSM_LANES), b_ab_im=ab_im.reshape(n_b, 1, SSM_LANES),
        b_wc_re=_block_diag(b_c_re.transpose(0, 1, 3, 2).astype(BF16), STATE, GROUP),
        b_wc_im=_block_diag(b_c_im.transpose(0, 1, 3, 2).astype(BF16), STATE, GROUP),
        b_d=b_d.reshape(n_b, 1, D_MODEL), b_w_glu=b_w_glu,
        x_wq=x_wq, x_wo=x_wo,
    )

    mem_k, mem_v, mem_kb, mem_vb = _mem_kv(mem_prompt, vec(norm_mem), x_wk, x_wv)
    y_prompt, y_sample, p_re, p_im, s_re, s_im, s_v = _trunk(
        x_prompt.reshape(batch * seq, D_MODEL), x_sample.reshape(dec_batch, D_MODEL), batch, seq,
        mem_kb, mem_vb, cache_mem_k, cache_mem_v,
        state_ssm_re.reshape(n_b, dec_batch, SSM_LANES),
        state_ssm_im.reshape(n_b, dec_batch, SSM_LANES), w)

    def states(parts, n):
        return jnp.stack(parts).reshape(n_b, n, N_GROUPS, STATE)

    return (y_prompt.reshape(batch, seq, D_MODEL),
            y_sample.reshape(dec_batch, 1, D_MODEL),
            mem_k, mem_v,
            states(p_re, batch), states(p_im, batch),
            states(s_re, dec_batch), states(s_im, dec_batch),
            jnp.stack(s_v).reshape(n_a, dec_batch, 1, D_A))
```

```python
import functools
import math

import jax
import jax.numpy as jnp
from jax import lax
from jax.experimental import pallas as pl
from jax.experimental.pallas import tpu as pltpu

F32 = jnp.float32
BF16 = jnp.bfloat16

D_MODEL = 1024
DEPTH = 4
CHUNK = 128
D_A = 2 * D_MODEL
A_HEADS = 8
A_HEAD_DIM = D_A // A_HEADS
GROUP = 16
N_GROUPS = D_MODEL // GROUP
STATE = 64
SSM_LANES = N_GROUPS * STATE
N_MEM = 256
X_HEADS = 4
X_HEAD_DIM = D_MODEL // X_HEADS
D_FF = 2816
EPS = 1e-6

V7X_SUBLANES = 8
V7X_LANES = 128
V7X_VMEM_LIMIT_BYTES = 56 * 1024 * 1024

ROW_TILE = 512
FFN_ROW_TILE = 512
FFN_FF_CHUNK = 768
SSM_TIME_TILE = 64
SSM_GROUP_BLOCK = 16
SSM_BLOCKS = N_GROUPS // SSM_GROUP_BLOCK
SCAN_LANES = 512
MEM_KV_SEQS = 2


def _params(*semantics):
    return pltpu.CompilerParams(dimension_semantics=semantics,
                                vmem_limit_bytes=V7X_VMEM_LIMIT_BYTES)


def _const_spec(shape):
    nd = len(shape)
    return pl.BlockSpec(shape, lambda *_: (0,) * nd, pipeline_mode=pl.Buffered(1))


def _layer_spec(shape, layer):
    nd = len(shape)
    return pl.BlockSpec((None,) + tuple(shape), lambda *_: (layer,) + (0,) * nd,
                        pipeline_mode=pl.Buffered(1))


def _rmsnorm(x, g):
    return x * lax.rsqrt(jnp.mean(x * x, axis=-1, keepdims=True) + EPS) * g


def _dot(a, b):
    return lax.dot_general(a, b, (((1,), (0,)), ((), ())), preferred_element_type=F32)


def _ffn_kernel(x_ref, xs_ref, g_ref, wg_hbm, wu_hbm, wd_hbm, gf_ref, o_ref, os_ref,
                wg_ref, wu_ref, wd_ref, sem, *, layer, final_norm):
    step = pl.program_id(0)
    chunks = [(c0, min(FFN_FF_CHUNK, D_FF - c0)) for c0 in range(0, D_FF, FFN_FF_CHUNK)]

    def copies(c):
        c0, w = chunks[c]
        return (pltpu.make_async_copy(wg_hbm.at[layer, :, pl.ds(c0, w)], wg_ref.at[:, pl.ds(c0, w)],
                                      sem.at[0, c]),
                pltpu.make_async_copy(wu_hbm.at[layer, :, pl.ds(c0, w)], wu_ref.at[:, pl.ds(c0, w)],
                                      sem.at[1, c]),
                pltpu.make_async_copy(wd_hbm.at[layer, pl.ds(c0, w), :], wd_ref.at[pl.ds(c0, w), :],
                                      sem.at[2, c]))

    def wait_chunk(c):
        for cp in copies(c):
            cp.wait()

    def ffn_rows(x, before_chunk=None):
        h = _rmsnorm(x, g_ref[...]).astype(BF16)
        down = None
        for c, (c0, w) in enumerate(chunks):
            if before_chunk is not None:
                before_chunk(c)
            cols = slice(c0, c0 + w)
            gate = _dot(h, wg_ref[:, cols])
            up = _dot(h, wu_ref[:, cols])
            act = (jax.nn.silu(gate) * up).astype(BF16)
            part = _dot(act, wd_ref[cols, :])
            down = part if down is None else down + part
        y = x + 0.5 * down
        return _rmsnorm(y, gf_ref[...]) if final_norm else y

    @pl.when(step == 0)
    def _():
        for c in range(len(chunks)):
            for cp in copies(c):
                cp.start()
        o_ref[...] = ffn_rows(x_ref[...], wait_chunk)

    @pl.when(step != 0)
    def _():
        o_ref[...] = ffn_rows(x_ref[...])

    @pl.when(step == pl.num_programs(0) - 1)
    def _():
        os_ref[...] = ffn_rows(xs_ref[...])


def _ffn(x, xs, g, wg, wu, wd, g_final, layer, final_norm):
    rows = x.shape[0]
    tm = FFN_ROW_TILE
    assert rows // tm > 1
    row_spec = pl.BlockSpec((tm, D_MODEL), lambda i: (i, 0))
    hbm_spec = pl.BlockSpec(memory_space=pl.ANY)
    n_chunks = pl.cdiv(D_FF, FFN_FF_CHUNK)
    return pl.pallas_call(
        functools.partial(_ffn_kernel, layer=layer, final_norm=final_norm),
        out_shape=(jax.ShapeDtypeStruct(x.shape, F32), jax.ShapeDtypeStruct(xs.shape, F32)),
        grid=(rows // tm,),
        in_specs=[row_spec,
                  _const_spec(xs.shape),
                  _layer_spec((1, D_MODEL), layer),
                  hbm_spec, hbm_spec, hbm_spec,
                  _const_spec((1, D_MODEL))],
        out_specs=(row_spec, _const_spec(xs.shape)),
        scratch_shapes=[pltpu.VMEM((D_MODEL, D_FF), F32), pltpu.VMEM((D_MODEL, D_FF), F32),
                        pltpu.VMEM((D_FF, D_MODEL), F32),
                        pltpu.SemaphoreType.DMA((3, n_chunks))],
        compiler_params=_params("arbitrary"),
        name="ffn",
    )(x, xs, g, wg, wu, wd, g_final)


def _gated_unit_inputs(x_ref, g_ref, w_in_ref, ln_g_ref, ln_b_ref):
    h = _rmsnorm(x_ref[...], g_ref[...]).astype(BF16)
    uv = jax.nn.gelu(_dot(h, w_in_ref[...]))
    u = uv[:, :D_A]
    v = uv[:, D_A:]
    mu = jnp.mean(v, axis=-1, keepdims=True)
    vc = v - mu
    var = jnp.mean(vc * vc, axis=-1, keepdims=True)
    v = vc * lax.rsqrt(var + EPS) * ln_g_ref[...] + ln_b_ref[...]
    return u, v


def _mixer_a_kernel(x_ref, xs_ref, g_ref, w_in_ref, ln_g_ref, ln_b_ref, ws_ref, bias_ref,
                    scale0_ref, bias0_ref, w_out_ref, o_ref, os_ref, vs_ref, gated_ref):
    u, v = _gated_unit_inputs(x_ref, g_ref, w_in_ref, ln_g_ref, ln_b_ref)
    vb = v.astype(BF16)
    t_idx = lax.broadcasted_iota(jnp.int32, (CHUNK, CHUNK), 0)
    s_idx = lax.broadcasted_iota(jnp.int32, (CHUNK, CHUNK), 1)
    causal = t_idx >= s_idx
    for head in range(A_HEADS):
        cols = slice(head * A_HEAD_DIM, (head + 1) * A_HEAD_DIM)
        ws = jnp.where(causal, ws_ref[head], 0.0).astype(BF16)
        bias = bias_ref[:, cols]
        for c in range(x_ref.shape[0] // CHUNK):
            rows = slice(c * CHUNK, (c + 1) * CHUNK)
            mixed = _dot(ws, vb[rows, cols]) + bias
            gated_ref[rows, cols] = (u[rows, cols] * mixed).astype(BF16)
    o_ref[...] = x_ref[...] + _dot(gated_ref[...], w_out_ref[...])

    @pl.when(pl.program_id(0) == pl.num_programs(0) - 1)
    def _():
        us, vs = _gated_unit_inputs(xs_ref, g_ref, w_in_ref, ln_g_ref, ln_b_ref)
        vs_ref[...] = vs
        mixed = vs * scale0_ref[...] + bias0_ref[...]
        os_ref[...] = xs_ref[...] + _dot((us * mixed).astype(BF16), w_out_ref[...])


def _mixer_a(x, xs, g, w_in, ln_g, ln_b, w_s, bias_full, scale0, bias0, w_out, layer, j):
    rows = x.shape[0]
    tm = ROW_TILE
    row_spec = pl.BlockSpec((tm, D_MODEL), lambda i: (i, 0))
    vs_shape = (xs.shape[0], D_A)
    return pl.pallas_call(
        _mixer_a_kernel,
        out_shape=(jax.ShapeDtypeStruct(x.shape, F32), jax.ShapeDtypeStruct(xs.shape, F32),
                   jax.ShapeDtypeStruct(vs_shape, F32)),
        grid=(rows // tm,),
        in_specs=[row_spec,
                  _const_spec(xs.shape),
                  _layer_spec((1, D_MODEL), layer),
                  _layer_spec((D_MODEL, 2 * D_A), j),
                  _layer_spec((1, D_A), j),
                  _layer_spec((1, D_A), j),
                  _layer_spec((A_HEADS, CHUNK, CHUNK), j),
                  _layer_spec((CHUNK, D_A), j),
                  _layer_spec((1, D_A), j),
                  _layer_spec((1, D_A), j),
                  _layer_spec((D_A, D_MODEL), j)],
        out_specs=(row_spec, _const_spec(xs.shape), _const_spec(vs_shape)),
        scratch_shapes=[pltpu.VMEM((tm, D_A), BF16)],
        compiler_params=_params("arbitrary"),
        name="mixer_a",
    )(x, xs, g, w_in, ln_g, ln_b, w_s, bias_full, scale0, bias0, w_out)


def _ssm_discretize_kernel(a_re_ref, a_im_ref, log_dt_ref, b_re_ref, b_im_ref,
                           ab_re_ref, ab_im_ref, bb_re_ref, bb_im_ref):
    a_re = a_re_ref[...]
    a_im = a_im_ref[...]
    dt = jnp.exp(log_dt_ref[...])
    mag = jnp.exp(a_re * dt)
    ab_re = mag * jnp.cos(a_im * dt)
    ab_im = mag * jnp.sin(a_im * dt)
    den = a_re * a_re + a_im * a_im
    nr = ab_re - 1.0
    ni = ab_im
    f_re = (nr * a_re + ni * a_im) / den
    f_im = (ni * a_re - nr * a_im) / den
    b_re = b_re_ref[...]
    b_im = b_im_ref[...]
    ab_re_ref[...] = ab_re
    ab_im_ref[...] = ab_im
    bb_re_ref[...] = f_re * b_re - f_im * b_im
    bb_im_ref[...] = f_re * b_im + f_im * b_re


def _ssm_discretize(a_re, a_im, log_dt, b_re_t, b_im_t):
    n_layers = a_re.shape[0]
    ab_shape = jax.ShapeDtypeStruct((n_layers, N_GROUPS, 1, STATE), F32)
    bb_shape = jax.ShapeDtypeStruct((n_layers, N_GROUPS, GROUP, STATE), F32)
    return pl.pallas_call(
        _ssm_discretize_kernel,
        out_shape=(ab_shape, ab_shape, bb_shape, bb_shape),
        name="ssm_discretize",
    )(a_re.reshape(ab_shape.shape), a_im.reshape(ab_shape.shape),
      log_dt.reshape(n_layers, N_GROUPS, 1, 1), b_re_t, b_im_t)


def _ssm_kernel(x_ref, g_ref, wb_ref, ab_re_ref, ab_im_ref, wc_re_ref, wc_im_ref, d_ref, wglu_ref,
                h0_re_ref, h0_im_ref, o_ref, hf_re_ref, hf_im_ref,
                s_re_ref, s_im_ref, st_re_ref, st_im_ref, tm_ref, abb_re_ref, abb_im_ref,
                *, n_batch, n_time):
    step = pl.program_id(0)
    blk = SSM_GROUP_BLOCK * STATE
    cblk = SSM_GROUP_BLOCK * GROUP
    lane_tiles = [slice(j * V7X_LANES, (j + 1) * V7X_LANES) for j in range(D_MODEL // V7X_LANES)]

    @pl.when(step == 0)
    def _():
        st_re_ref[...] = h0_re_ref[...]
        st_im_ref[...] = h0_im_ref[...]
        abb_re_ref[...] = jnp.broadcast_to(ab_re_ref[...], abb_re_ref.shape)
        abb_im_ref[...] = jnp.broadcast_to(ab_im_ref[...], abb_im_ref.shape)

    if n_time == 1:
        u = _rmsnorm(x_ref[...], g_ref[...])
    else:
        for b in range(n_batch):
            ub_rows = _rmsnorm(x_ref[b], g_ref[...])
            for j, cols in enumerate(lane_tiles):
                tm_ref[j, pl.ds(b, n_time, stride=n_batch), :] = ub_rows[:, cols]
        u = jnp.concatenate([tm_ref[j] for j in range(len(lane_tiles))], axis=1)
    ub = u.astype(BF16)

    def project_in(k):
        bu = _dot(ub[:, k * cblk:(k + 1) * cblk], wb_ref[k])
        s_re_ref[:, k * blk:(k + 1) * blk] = bu[:, :blk]
        s_im_ref[:, k * blk:(k + 1) * blk] = bu[:, blk:]

    def scan(c):
        lanes = slice(c * SCAN_LANES, (c + 1) * SCAN_LANES)
        ar = abb_re_ref[:, lanes]
        ai = abb_im_ref[:, lanes]
        for s in range(n_batch // V7X_SUBLANES):
            rows0 = slice(s * V7X_SUBLANES, (s + 1) * V7X_SUBLANES)
            hr = st_re_ref[rows0, lanes]
            hi = st_im_ref[rows0, lanes]
            for t in range(n_time):
                rows = slice(t * n_batch + s * V7X_SUBLANES, t * n_batch + (s + 1) * V7X_SUBLANES)
                hr, hi = (ar * hr - ai * hi + s_re_ref[rows, lanes],
                          ar * hi + ai * hr + s_im_ref[rows, lanes])
                s_re_ref[rows, lanes] = hr
                s_im_ref[rows, lanes] = hi
            st_re_ref[rows0, lanes] = hr
            st_im_ref[rows0, lanes] = hi

    def project_out(k):
        hr = s_re_ref[:, k * blk:(k + 1) * blk].astype(BF16)
        hi = s_im_ref[:, k * blk:(k + 1) * blk].astype(BF16)
        return _dot(hr, wc_re_ref[k]) - _dot(hi, wc_im_ref[k])

    ys = []
    if n_time == 1:
        for k in range(SSM_BLOCKS):
            project_in(k)
        ar = ab_re_ref[...]
        ai = ab_im_ref[...]
        hr0 = st_re_ref[...]
        hi0 = st_im_ref[...]
        hr = ar * hr0 - ai * hi0 + s_re_ref[...]
        hi = ar * hi0 + ai * hr0 + s_im_ref[...]
        s_re_ref[...] = hr
        s_im_ref[...] = hi
        st_re_ref[...] = hr
        st_im_ref[...] = hi
        for k in range(SSM_BLOCKS):
            ys.append(project_out(k))
    else:
        chunks_per_block = blk // SCAN_LANES
        project_in(0)
        for k in range(SSM_BLOCKS):
            if k + 1 < SSM_BLOCKS:
                project_in(k + 1)
            for c in range(chunks_per_block):
                scan(k * chunks_per_block + c)
            ys.append(project_out(k))
    y = jnp.concatenate(ys, axis=-1) + d_ref[...] * u
    z = jax.nn.gelu(y).astype(BF16)
    vg = _dot(z, wglu_ref[...])
    out = vg[:, :D_MODEL] * jax.nn.sigmoid(vg[:, D_MODEL:])
    if n_time == 1:
        o_ref[...] = x_ref[...] + out
    else:
        for j, cols in enumerate(lane_tiles):
            tm_ref[j] = out[:, cols]
        for b in range(n_batch):
            rows = pl.ds(b, n_time, stride=n_batch)
            out_b = jnp.concatenate([tm_ref[j, rows, :] for j in range(len(lane_tiles))], axis=1)
            o_ref[b] = x_ref[b] + out_b

    @pl.when(step == pl.num_programs(0) - 1)
    def _():
        hf_re_ref[...] = st_re_ref[...]
        hf_im_ref[...] = st_im_ref[...]


def _ssm_mixer(x, g, wb, ab_re, ab_im, wc_re, wc_im, d_skip, w_glu, h0_re, h0_im,
               layer, j, n_batch, n_time_total):
    n_time = min(SSM_TIME_TILE, n_time_total)
    tile = n_time * n_batch
    if n_time_total == 1:
        x_in = x
        row_spec = _const_spec((n_batch, D_MODEL))
    else:
        x_in = x.reshape(n_batch, n_time_total, D_MODEL)
        row_spec = pl.BlockSpec((n_batch, n_time, D_MODEL), lambda i: (0, i, 0))
    state_spec = _const_spec((n_batch, SSM_LANES))
    state_shape = jax.ShapeDtypeStruct((n_batch, SSM_LANES), F32)
    blk = SSM_GROUP_BLOCK * STATE
    cblk = SSM_GROUP_BLOCK * GROUP
    out, h_re, h_im = pl.pallas_call(
        functools.partial(_ssm_kernel, n_batch=n_batch, n_time=n_time),
        out_shape=(jax.ShapeDtypeStruct(x_in.shape, F32), state_shape, state_shape),
        grid=(n_time_total // n_time,),
        in_specs=[row_spec,
                  _layer_spec((1, D_MODEL), layer),
                  _layer_spec((SSM_BLOCKS, cblk, 2 * blk), j),
                  _layer_spec((1, SSM_LANES), j),
                  _layer_spec((1, SSM_LANES), j),
                  _layer_spec((SSM_BLOCKS, blk, cblk), j),
                  _layer_spec((SSM_BLOCKS, blk, cblk), j),
                  _layer_spec((1, D_MODEL), j),
                  _layer_spec((D_MODEL, 2 * D_MODEL), j),
                  state_spec, state_spec],
        out_specs=(row_spec, state_spec, state_spec),
        scratch_shapes=[pltpu.VMEM((tile, SSM_LANES), F32), pltpu.VMEM((tile, SSM_LANES), F32),
                        pltpu.VMEM((n_batch, SSM_LANES), F32), pltpu.VMEM((n_batch, SSM_LANES), F32),
                        pltpu.VMEM((D_MODEL // V7X_LANES, tile, V7X_LANES), F32),
                        pltpu.VMEM((V7X_SUBLANES, SSM_LANES), F32),
                        pltpu.VMEM((V7X_SUBLANES, SSM_LANES), F32)],
        compiler_params=_params("arbitrary"),
        name="ssm_mixer",
    )(x_in, g, wb, ab_re, ab_im, wc_re, wc_im, d_skip, w_glu, h0_re, h0_im)
    return out.reshape(x.shape), h_re, h_im


def _block_diag(w, rows_inner, cols_inner):
    n_layers = w.shape[0]
    rows, cols = SSM_GROUP_BLOCK * rows_inner, SSM_GROUP_BLOCK * cols_inner
    tiled = jnp.tile(w.reshape(n_layers, SSM_BLOCKS, rows, cols_inner), (1, 1, 1, SSM_GROUP_BLOCK))
    row_group = lax.broadcasted_iota(jnp.int32, (rows, cols), 0) // rows_inner
    col_group = lax.broadcasted_iota(jnp.int32, (rows, cols), 1) // cols_inner
    return jnp.where(row_group == col_group, tiled, jnp.zeros_like(tiled))


def _mem_kv_kernel(m_ref, g_ref, wk_ref, wv_ref, k_ref, v_ref, kb_ref, vb_ref):
    m = _rmsnorm(m_ref[...], g_ref[...]).astype(BF16)
    k = _dot(m, wk_ref[...])
    v = _dot(m, wv_ref[...])
    kb_ref[...] = k.astype(BF16)
    vb_ref[...] = v.astype(BF16)
    for s in range(MEM_KV_SEQS):
        rows = slice(s * N_MEM, (s + 1) * N_MEM)
        for h in range(X_HEADS):
            cols = slice(h * X_HEAD_DIM, (h + 1) * X_HEAD_DIM)
            k_ref[s, :, h, :] = k[rows, cols]
            v_ref[s, :, h, :] = v[rows, cols]


def _mem_kv(mem, g, wk, wv):
    n_seq = mem.shape[0]
    tm = MEM_KV_SEQS * N_MEM
    out = jax.ShapeDtypeStruct((DEPTH, n_seq, N_MEM, X_HEADS, X_HEAD_DIM), F32)
    out_b = jax.ShapeDtypeStruct((DEPTH, n_seq * N_MEM, D_MODEL), BF16)
    w_spec = pl.BlockSpec((None, D_MODEL, D_MODEL), lambda l, i: (l, 0, 0))
    o_spec = pl.BlockSpec((None, MEM_KV_SEQS, N_MEM, X_HEADS, X_HEAD_DIM), lambda l, i: (l, i, 0, 0, 0))
    ob_spec = pl.BlockSpec((None, tm, D_MODEL), lambda l, i: (l, i, 0))
    return pl.pallas_call(
        _mem_kv_kernel,
        out_shape=(out, out, out_b, out_b),
        grid=(DEPTH, n_seq // MEM_KV_SEQS),
        in_specs=[pl.BlockSpec((tm, D_MODEL), lambda l, i: (i, 0)),
                  pl.BlockSpec((None, 1, D_MODEL), lambda l, i: (l, 0, 0)),
                  w_spec, w_spec],
        out_specs=(o_spec, o_spec, ob_spec, ob_spec),
        compiler_params=_params("parallel", "parallel"),
        name="mem_kv",
    )(mem.reshape(n_seq * N_MEM, D_MODEL), g, wk, wv)


def _softmax_rows(s):
    e = jnp.exp(s - jnp.max(s, axis=-1, keepdims=True))
    return e / jnp.sum(e, axis=-1, keepdims=True)


def _attn_kernel(x_ref, xs_ref, g_ref, wq_ref, k_ref, v_ref, ck_ref, cv_ref, wo_ref, o_ref, os_ref,
                 oh_ref, qs_ref, ohs_ref, *, nb):
    b, t = pl.program_id(0), pl.program_id(1)
    step = b * pl.num_programs(1) + t
    last = pl.num_programs(0) * pl.num_programs(1) - 1
    scale = X_HEAD_DIM ** -0.5
    head_cols = [slice(h * X_HEAD_DIM, (h + 1) * X_HEAD_DIM) for h in range(X_HEADS)]

    @pl.when(step == 0)
    def _():
        hs = _rmsnorm(xs_ref[...], g_ref[...]).astype(BF16)
        qs_ref[...] = _dot(hs, wq_ref[...])
        ohs_ref[...] = jnp.zeros_like(ohs_ref)

    x = x_ref[...]
    h = _rmsnorm(x, g_ref[...]).astype(BF16)
    q = _dot(h, wq_ref[...])
    for cols in head_cols:
        s = lax.dot_general(q[:, cols].astype(BF16), k_ref[:, cols], (((1,), (1,)), ((), ())),
                            preferred_element_type=F32) * scale
        p = _softmax_rows(s).astype(BF16)
        oh_ref[:, cols] = _dot(p, v_ref[:, cols]).astype(BF16)
    o_ref[...] = x + _dot(oh_ref[...], wo_ref[...])

    parts = V7X_SUBLANES // nb
    sub = step % parts
    row0 = pl.multiple_of((step // parts) * V7X_SUBLANES, V7X_SUBLANES)
    q_group = qs_ref[pl.ds(row0, V7X_SUBLANES), :]
    q_rows = q_group[:nb, :]
    for c in range(1, parts):
        q_rows = jnp.where(sub == c, q_group[c * nb:(c + 1) * nb, :], q_rows)
    n_rows = N_MEM * X_HEADS
    head_of_lane = lax.broadcasted_iota(jnp.int32, (V7X_SUBLANES, n_rows), 1) % X_HEADS
    head_of_row = lax.broadcasted_iota(jnp.int32, (V7X_SUBLANES, n_rows), 0) % X_HEADS
    own_head = head_of_lane == head_of_row
    scores = []
    for r in range(nb):
        qr = q_rows[r:r + 1, :]
        qh = jnp.concatenate([qr[:, c] for c in head_cols] * (V7X_SUBLANES // X_HEADS), axis=0)
        k2 = ck_ref[r].reshape(n_rows, X_HEAD_DIM).astype(BF16)
        scores.append(lax.dot_general(qh.astype(BF16), k2, (((1,), (1,)), ((), ())),
                                      preferred_element_type=F32))
    sc = jnp.concatenate(scores, axis=0) * scale
    own = jnp.concatenate([own_head] * nb, axis=0)
    p_all = _softmax_rows(jnp.where(own, sc, -jnp.inf)).astype(BF16)
    outs = []
    for r in range(nb):
        v2 = cv_ref[r].reshape(n_rows, X_HEAD_DIM).astype(BF16)
        pv = _dot(p_all[r * V7X_SUBLANES:(r + 1) * V7X_SUBLANES, :], v2)
        outs.append(jnp.concatenate([pv[hd:hd + 1, :] for hd in range(X_HEADS)], axis=1))
    o_group = jnp.concatenate(outs * parts, axis=0)
    part_of_row = lax.broadcasted_iota(jnp.int32, (V7X_SUBLANES, D_MODEL), 0) // nb
    rows = pl.ds(row0, V7X_SUBLANES)
    ohs_ref[rows, :] = jnp.where(part_of_row == sub, o_group, ohs_ref[rows, :])

    @pl.when(step == last)
    def _():
        os_ref[...] = xs_ref[...] + _dot(ohs_ref[...].astype(BF16), wo_ref[...])


def _attn(x, xs, g, wq, k, v, cache_k, cache_v, wo, layer, n_seq, seq_len):
    tm = ROW_TILE
    n_t = seq_len // tm
    nb, rem = divmod(xs.shape[0], n_seq * n_t)
    assert rem == 0 and V7X_SUBLANES % nb == 0
    row_spec = pl.BlockSpec((tm, D_MODEL), lambda b, i: (b * n_t + i, 0))
    kv_spec = pl.BlockSpec((None, N_MEM, D_MODEL), lambda b, i: (layer, b, 0))
    cache_spec = pl.BlockSpec((None, nb, N_MEM, X_HEADS, X_HEAD_DIM),
                              lambda b, i: (layer, b * n_t + i, 0, 0, 0))
    return pl.pallas_call(
        functools.partial(_attn_kernel, nb=nb),
        out_shape=(jax.ShapeDtypeStruct(x.shape, F32), jax.ShapeDtypeStruct(xs.shape, F32)),
        grid=(n_seq, n_t),
        in_specs=[row_spec,
                  _const_spec(xs.shape),
                  _layer_spec((1, D_MODEL), layer),
                  _layer_spec((D_MODEL, D_MODEL), layer),
                  kv_spec, kv_spec, cache_spec, cache_spec,
                  _layer_spec((D_MODEL, D_MODEL), layer)],
        out_specs=(row_spec, _const_spec(xs.shape)),
        scratch_shapes=[pltpu.VMEM((tm, D_MODEL), BF16),
                        pltpu.VMEM(xs.shape, F32), pltpu.VMEM(xs.shape, F32)],
        compiler_params=_params("arbitrary", "arbitrary"),
        name="attn",
    )(x, xs, g, wq, k, v, cache_k, cache_v, wo)


def _trunk(x, xs, n_seq, seq_len, mem_k, mem_v, cache_k, cache_v, state_re, state_im, w):
    n_samples = xs.shape[0]
    zero_state = jnp.zeros((n_seq, SSM_LANES), F32)
    p_re, p_im, s_re, s_im, s_v = [], [], [], [], []
    for i in range(DEPTH):
        j = i // 2
        x, xs = _ffn(x, xs, w['norm_ffn1'], w['ffn1_wg'], w['ffn1_wu'], w['ffn1_wd'], w['norm_final'],
                     i, False)
        if i % 2 == 0:
            x, xs, v_rows = _mixer_a(x, xs, w['norm_mix'], w['a_w_in'], w['a_ln_g'], w['a_ln_b'],
                                     w['a_w_s'], w['a_bias_full'], w['a_scale0'], w['a_bias0'],
                                     w['a_w_out'], i, j)
            s_v.append(v_rows)
        else:
            ssm_w = (w['norm_mix'], w['b_wb'], w['b_ab_re'], w['b_ab_im'], w['b_wc_re'], w['b_wc_im'],
                     w['b_d'], w['b_w_glu'])
            x, hr, hi = _ssm_mixer(x, *ssm_w, zero_state, zero_state, i, j, n_seq, seq_len)
            p_re.append(hr)
            p_im.append(hi)
            xs, hr, hi = _ssm_mixer(xs, *ssm_w, state_re[j], state_im[j], i, j, n_samples, 1)
            s_re.append(hr)
            s_im.append(hi)
        x, xs = _attn(x, xs, w['norm_x'], w['x_wq'], mem_k, mem_v, cache_k, cache_v, w['x_wo'],
                      i, n_seq, seq_len)
        x, xs = _ffn(x, xs, w['norm_ffn2'], w['ffn2_wg'], w['ffn2_wu'], w['ffn2_wd'], w['norm_final'],
                     i, i == DEPTH - 1)
    return x, xs, p_re, p_im, s_re, s_im, s_v


def kernel(x_prompt, x_sample, cache_mem_k, cache_mem_v, state_ssm_re, state_ssm_im, mem_prompt,
           norm_ffn1, ffn1_wg, ffn1_wu, ffn1_wd, norm_mix,
           a_w_in, a_ln_g, a_ln_b, a_w_s, a_b_s, a_w_out,
           b_a_re, b_a_im, b_log_dt, b_b_re, b_b_im, b_c_re, b_c_im, b_d, b_w_glu,
           norm_x, norm_mem, x_wq, x_wk, x_wv, x_wo,
           norm_ffn2, ffn2_wg, ffn2_wu, ffn2_wd, norm_final):
    batch, seq, _ = x_prompt.shape
    dec_batch, dec_seq, _ = x_sample.shape
    assert dec_seq == 1 and seq % ROW_TILE == 0 and ROW_TILE % CHUNK == 0
    n_a = a_w_in.shape[0]
    n_b = b_a_re.shape[0]

    def vec(p):
        return p.reshape(p.shape[0], 1, p.shape[-1])

    ab_re, ab_im, bb_re, bb_im = _ssm_discretize(
        b_a_re, b_a_im, b_log_dt, b_b_re.transpose(0, 1, 3, 2), b_b_im.transpose(0, 1, 3, 2))
    wb = jnp.concatenate([_block_diag(bb_re.astype(BF16), GROUP, STATE),
                          _block_diag(bb_im.astype(BF16), GROUP, STATE)], axis=-1)
    w = dict(
        norm_ffn1=vec(norm_ffn1), norm_ffn2=vec(norm_ffn2), norm_mix=vec(norm_mix), norm_x=vec(norm_x),
        norm_final=norm_final.reshape(1, D_MODEL),
        ffn1_wg=ffn1_wg, ffn1_wu=ffn1_wu, ffn1_wd=ffn1_wd,
        ffn2_wg=ffn2_wg, ffn2_wu=ffn2_wu, ffn2_wd=ffn2_wd,
        a_w_in=a_w_in, a_w_out=a_w_out,
        a_ln_g=vec(a_ln_g), a_ln_b=vec(a_ln_b), a_w_s=a_w_s,
        a_bias_full=jnp.repeat(jnp.swapaxes(a_b_s, 1, 2), A_HEAD_DIM, axis=-1),
        a_scale0=jnp.repeat(a_w_s[:, :, 0, 0], A_HEAD_DIM, axis=-1).reshape(n_a, 1, D_A),
        a_bias0=jnp.repeat(a_b_s[:, :, 0], A_HEAD_DIM, axis=-1).reshape(n_a, 1, D_A),
        b_wb=wb,
        b_ab_re=ab_re.reshape(n_b, 1, SSM_LANES), b_ab_im=ab_im.reshape(n_b, 1, SSM_LANES),
        b_wc_re=_block_diag(b_c_re.transpose(0, 1, 3, 2).astype(BF16), STATE, GROUP),
        b_wc_im=_block_diag(b_c_im.transpose(0, 1, 3, 2).astype(BF16), STATE, GROUP),
        b_d=b_d.reshape(n_b, 1, D_MODEL), b_w_glu=b_w_glu,
        x_wq=x_wq, x_wo=x_wo,
    )

    mem_k, mem_v, mem_kb, mem_vb = _mem_kv(mem_prompt, vec(norm_mem), x_wk, x_wv)
    y_prompt, y_sample, p_re, p_im, s_re, s_im, s_v = _trunk(
        x_prompt.reshape(batch * seq, D_MODEL), x_sample.reshape(dec_batch, D_MODEL), batch, seq,
        mem_kb, mem_vb, cache_mem_k, cache_mem_v,
        state_ssm_re.reshape(n_b, dec_batch, SSM_LANES),
        state_ssm_im.reshape(n_b, dec_batch, SSM_LANES), w)

    def states(parts, n):
        return jnp.stack(parts).reshape(n_b, n, N_GROUPS, STATE)

    return (y_prompt.reshape(batch, seq, D_MODEL),
            y_sample.reshape(dec_batch, 1, D_MODEL),
            mem_k, mem_v,
            states(p_re, batch), states(p_im, batch),
            states(s_re, dec_batch), states(s_im, dec_batch),
            jnp.stack(s_v).reshape(n_a, dec_batch, 1, D_A))
```

```python
import functools
import math

import jax
import jax.numpy as jnp
from jax import lax
from jax.experimental import pallas as pl
from jax.experimental.pallas import tpu as pltpu

F32 = jnp.float32
BF16 = jnp.bfloat16

D_MODEL = 1024
DEPTH = 4
CHUNK = 128
D_A = 2 * D_MODEL
A_HEADS = 8
A_HEAD_DIM = D_A // A_HEADS
GROUP = 16
N_GROUPS = D_MODEL // GROUP
STATE = 64
SSM_LANES = N_GROUPS * STATE
N_MEM = 256
X_HEADS = 4
X_HEAD_DIM = D_MODEL // X_HEADS
D_FF = 2816
EPS = 1e-6

V7X_SUBLANES = 8
V7X_LANES = 128
V7X_VMEM_LIMIT_BYTES = 56 * 1024 * 1024

ROW_TILE = 512
FFN_ROW_TILE = 512
FFN_FF_CHUNK = 768
SSM_TIME_TILE = 64
SSM_GROUP_BLOCK = 16
SSM_BLOCKS = N_GROUPS // SSM_GROUP_BLOCK
SCAN_LANES = 512
MEM_KV_SEQS = 2


def _params(*semantics):
    return pltpu.CompilerParams(dimension_semantics=semantics,
                                vmem_limit_bytes=V7X_VMEM_LIMIT_BYTES)


def _const_spec(shape):
    nd = len(shape)
    return pl.BlockSpec(shape, lambda *_: (0,) * nd, pipeline_mode=pl.Buffered(1))


def _layer_spec(shape, layer):
    nd = len(shape)
    return pl.BlockSpec((None,) + tuple(shape), lambda *_: (layer,) + (0,) * nd,
                        pipeline_mode=pl.Buffered(1))


def _rmsnorm(x, g):
    return x * lax.rsqrt(jnp.mean(x * x, axis=-1, keepdims=True) + EPS) * g


def _dot(a, b):
    return lax.dot_general(a, b, (((1,), (0,)), ((), ())), preferred_element_type=F32)


def _ffn_kernel(x_ref, xs_ref, g_ref, wg_hbm, wu_hbm, wd_hbm, gf_ref, o_ref, os_ref,
                wg_ref, wu_ref, wd_ref, sem, *, layer, final_norm):
    step = pl.program_id(0)
    chunks = [(c0, min(FFN_FF_CHUNK, D_FF - c0)) for c0 in range(0, D_FF, FFN_FF_CHUNK)]

    def copies(c):
        c0, w = chunks[c]
        return (pltpu.make_async_copy(wg_hbm.at[layer, :, pl.ds(c0, w)], wg_ref.at[:, pl.ds(c0, w)],
                                      sem.at[0, c]),
                pltpu.make_async_copy(wu_hbm.at[layer, :, pl.ds(c0, w)], wu_ref.at[:, pl.ds(c0, w)],
                                      sem.at[1, c]),
                pltpu.make_async_copy(wd_hbm.at[layer, pl.ds(c0, w), :], wd_ref.at[pl.ds(c0, w), :],
                                      sem.at[2, c]))

    def wait_chunk(c):
        for cp in copies(c):
            cp.wait()

    def ffn_rows(x, before_chunk=None):
        h = _rmsnorm(x, g_ref[...]).astype(BF16)
        down = None
        for c, (c0, w) in enumerate(chunks):
            if before_chunk is not None:
                before_chunk(c)
            cols = slice(c0, c0 + w)
            gate = _dot(h, wg_ref[:, cols])
            up = _dot(h, wu_ref[:, cols])
            act = (jax.nn.silu(gate) * up).astype(BF16)
            part = _dot(act, wd_ref[cols, :])
            down = part if down is None else down + part
        y = x + 0.5 * down
        return _rmsnorm(y, gf_ref[...]) if final_norm else y

    @pl.when(step == 0)
    def _():
        for c in range(len(chunks)):
            for cp in copies(c):
                cp.start()
        o_ref[...] = ffn_rows(x_ref[...], wait_chunk)

    @pl.when(step != 0)
    def _():
        o_ref[...] = ffn_rows(x_ref[...])

    @pl.when(step == pl.num_programs(0) - 1)
    def _():
        os_ref[...] = ffn_rows(xs_ref[...])


def _ffn(x, xs, g, wg, wu, wd, g_final, layer, final_norm):
    rows = x.shape[0]
    tm = FFN_ROW_TILE
    assert rows // tm > 1
    row_spec = pl.BlockSpec((tm, D_MODEL), lambda i: (i, 0))
    hbm_spec = pl.BlockSpec(memory_space=pl.ANY)
    n_chunks = pl.cdiv(D_FF, FFN_FF_CHUNK)
    return pl.pallas_call(
        functools.partial(_ffn_kernel, layer=layer, final_norm=final_norm),
        out_shape=(jax.ShapeDtypeStruct(x.shape, F32), jax.ShapeDtypeStruct(xs.shape, F32)),
        grid=(rows // tm,),
        in_specs=[row_spec,
                  _const_spec(xs.shape),
                  _layer_spec((1, D_MODEL), layer),
                  hbm_spec, hbm_spec, hbm_spec,
                  _const_spec((1, D_MODEL))],
        out_specs=(row_spec, _const_spec(xs.shape)),
        scratch_shapes=[pltpu.VMEM((D_MODEL, D_FF), F32), pltpu.VMEM((D_MODEL, D_FF), F32),
                        pltpu.VMEM((D_FF, D_MODEL), F32),
                        pltpu.SemaphoreType.DMA((3, n_chunks))],
        compiler_params=_params("arbitrary"),
        name="ffn",
    )(x, xs, g, wg, wu, wd, g_final)


def _gated_unit_inputs(x_ref, g_ref, w_in_ref, ln_g_ref, ln_b_ref):
    h = _rmsnorm(x_ref[...], g_ref[...]).astype(BF16)
    uv = jax.nn.gelu(_dot(h, w_in_ref[...]))
    u = uv[:, :D_A]
    v = uv[:, D_A:]
    mu = jnp.mean(v, axis=-1, keepdims=True)
    vc = v - mu
    var = jnp.mean(vc * vc, axis=-1, keepdims=True)
    v = vc * lax.rsqrt(var + EPS) * ln_g_ref[...] + ln_b_ref[...]
    return u, v


def _mixer_a_kernel(x_ref, xs_ref, g_ref, w_in_ref, ln_g_ref, ln_b_ref, ws_ref, bias_ref,
                    scale0_ref, bias0_ref, w_out_ref, o_ref, os_ref, vs_ref, gated_ref):
    u, v = _gated_unit_inputs(x_ref, g_ref, w_in_ref, ln_g_ref, ln_b_ref)
    vb = v.astype(BF16)
    t_idx = lax.broadcasted_iota(jnp.int32, (CHUNK, CHUNK), 0)
    s_idx = lax.broadcasted_iota(jnp.int32, (CHUNK, CHUNK), 1)
    causal = t_idx >= s_idx
    for head in range(A_HEADS):
        cols = slice(head * A_HEAD_DIM, (head + 1) * A_HEAD_DIM)
        ws = jnp.where(causal, ws_ref[head], 0.0).astype(BF16)
        bias = bias_ref[:, cols]
        for c in range(x_ref.shape[0] // CHUNK):
            rows = slice(c * CHUNK, (c + 1) * CHUNK)
            mixed = _dot(ws, vb[rows, cols]) + bias
            gated_ref[rows, cols] = (u[rows, cols] * mixed).astype(BF16)
    o_ref[...] = x_ref[...] + _dot(gated_ref[...], w_out_ref[...])

    @pl.when(pl.program_id(0) == pl.num_programs(0) - 1)
    def _():
        us, vs = _gated_unit_inputs(xs_ref, g_ref, w_in_ref, ln_g_ref, ln_b_ref)
        vs_ref[...] = vs
        mixed = vs * scale0_ref[...] + bias0_ref[...]
        os_ref[...] = xs_ref[...] + _dot((us * mixed).astype(BF16), w_out_ref[...])


def _mixer_a(x, xs, g, w_in, ln_g, ln_b, w_s, bias_full, scale0, bias0, w_out, layer, j):
    rows = x.shape[0]
    tm = ROW_TILE
    row_spec = pl.BlockSpec((tm, D_MODEL), lambda i: (i, 0))
    vs_shape = (xs.shape[0], D_A)
    return pl.pallas_call(
        _mixer_a_kernel,
        out_shape=(jax.ShapeDtypeStruct(x.shape, F32), jax.ShapeDtypeStruct(xs.shape, F32),
                   jax.ShapeDtypeStruct(vs_shape, F32)),
        grid=(rows // tm,),
        in_specs=[row_spec,
                  _const_spec(xs.shape),
                  _layer_spec((1, D_MODEL), layer),
                  _layer_spec((D_MODEL, 2 * D_A), j),
                  _layer_spec((1, D_A), j),
                  _layer_spec((1, D_A), j),
                  _layer_spec((A_HEADS, CHUNK, CHUNK), j),
                  _layer_spec((CHUNK, D_A), j),
                  _layer_spec((1, D_A), j),
                  _layer_spec((1, D_A), j),
                  _layer_spec((D_A, D_MODEL), j)],
        out_specs=(row_spec, _const_spec(xs.shape), _const_spec(vs_shape)),
        scratch_shapes=[pltpu.VMEM((tm, D_A), BF16)],
        compiler_params=_params("arbitrary"),
        name="mixer_a",
    )(x, xs, g, w_in, ln_g, ln_b, w_s, bias_full, scale0, bias0, w_out)


def _ssm_discretize_kernel(a_re_ref, a_im_ref, log_dt_ref, b_re_ref, b_im_ref,
                           ab_re_ref, ab_im_ref, bb_re_ref, bb_im_ref):
    a_re = a_re_ref[...]
    a_im = a_im_ref[...]
    dt = jnp.exp(log_dt_ref[...])
    mag = jnp.exp(a_re * dt)
    ab_re = mag * jnp.cos(a_im * dt)
    ab_im = mag * jnp.sin(a_im * dt)
    den = a_re * a_re + a_im * a_im
    nr = ab_re - 1.0
    ni = ab_im
    f_re = (nr * a_re + ni * a_im) / den
    f_im = (ni * a_re - nr * a_im) / den
    b_re = b_re_ref[...]
    b_im = b_im_ref[...]
    ab_re_ref[...] = ab_re
    ab_im_ref[...] = ab_im
    bb_re_ref[...] = f_re * b_re - f_im * b_im
    bb_im_ref[...] = f_re * b_im + f_im * b_re


def _ssm_discretize(a_re, a_im, log_dt, b_re_t, b_im_t):
    n_layers = a_re.shape[0]
    ab_shape = jax.ShapeDtypeStruct((n_layers, N_GROUPS, 1, STATE), F32)
    bb_shape = jax.ShapeDtypeStruct((n_layers, N_GROUPS, GROUP, STATE), F32)
    return pl.pallas_call(
        _ssm_discretize_kernel,
        out_shape=(ab_shape, ab_shape, bb_shape, bb_shape),
        name="ssm_discretize",
    )(a_re.reshape(ab_shape.shape), a_im.reshape(ab_shape.shape),
      log_dt.reshape(n_layers, N_GROUPS, 1, 1), b_re_t, b_im_t)


def _ssm_kernel(x_ref, g_ref, wb_ref, ab_re_ref, ab_im_ref, wc_re_ref, wc_im_ref, d_ref, wglu_ref,
                h0_re_ref, h0_im_ref, o_ref, hf_re_ref, hf_im_ref,
                s_re_ref, s_im_ref, st_re_ref, st_im_ref, tm_ref, abb_re_ref, abb_im_ref,
                *, n_batch, n_time):
    step = pl.program_id(0)
    blk = SSM_GROUP_BLOCK * STATE
    cblk = SSM_GROUP_BLOCK * GROUP
    lane_tiles = [slice(j * V7X_LANES, (j + 1) * V7X_LANES) for j in range(D_MODEL // V7X_LANES)]

    @pl.when(step == 0)
    def _():
        st_re_ref[...] = h0_re_ref[...]
        st_im_ref[...] = h0_im_ref[...]
        abb_re_ref[...] = jnp.broadcast_to(ab_re_ref[...], abb_re_ref.shape)
        abb_im_ref[...] = jnp.broadcast_to(ab_im_ref[...], abb_im_ref.shape)

    if n_time == 1:
        u = _rmsnorm(x_ref[...], g_ref[...])
    else:
        for b in range(n_batch):
            ub_rows = _rmsnorm(x_ref[b], g_ref[...])
            for j, cols in enumerate(lane_tiles):
                tm_ref[j, pl.ds(b, n_time, stride=n_batch), :] = ub_rows[:, cols]
        u = jnp.concatenate([tm_ref[j] for j in range(len(lane_tiles))], axis=1)
    ub = u.astype(BF16)

    def project_in(k):
        bu = _dot(ub[:, k * cblk:(k + 1) * cblk], wb_ref[k])
        s_re_ref[:, k * blk:(k + 1) * blk] = bu[:, :blk]
        s_im_ref[:, k * blk:(k + 1) * blk] = bu[:, blk:]

    def scan(c):
        lanes = slice(c * SCAN_LANES, (c + 1) * SCAN_LANES)
        ar = abb_re_ref[:, lanes]
        ai = abb_im_ref[:, lanes]
        for s in range(n_batch // V7X_SUBLANES):
            rows0 = slice(s * V7X_SUBLANES, (s + 1) * V7X_SUBLANES)
            hr = st_re_ref[rows0, lanes]
            hi = st_im_ref[rows0, lanes]
            for t in range(n_time):
                rows = slice(t * n_batch + s * V7X_SUBLANES, t * n_batch + (s + 1) * V7X_SUBLANES)
                hr, hi = (ar * hr - ai * hi + s_re_ref[rows, lanes],
                          ar * hi + ai * hr + s_im_ref[rows, lanes])
                s_re_ref[rows, lanes] = hr
                s_im_ref[rows, lanes] = hi
            st_re_ref[rows0, lanes] = hr
            st_im_ref[rows0, lanes] = hi

    def project_out(k):
        hr = s_re_ref[:, k * blk:(k + 1) * blk].astype(BF16)
        hi = s_im_ref[:, k * blk:(k + 1) * blk].astype(BF16)
        return _dot(hr, wc_re_ref[k]) - _dot(hi, wc_im_ref[k])

    ys = []
    if n_time == 1:
        for k in range(SSM_BLOCKS):
            project_in(k)
        ar = ab_re_ref[...]
        ai = ab_im_ref[...]
        hr0 = st_re_ref[...]
        hi0 = st_im_ref[...]
        hr = ar * hr0 - ai * hi0 + s_re_ref[...]
        hi = ar * hi0 + ai * hr0 + s_im_ref[...]
        s_re_ref[...] = hr
        s_im_ref[...] = hi
        st_re_ref[...] = hr
        st_im_ref[...] = hi
        for k in range(SSM_BLOCKS):
            ys.append(project_out(k))
    else:
        chunks_per_block = blk // SCAN_LANES
        project_in(0)
        for k in range(SSM_BLOCKS):
            if k + 1 < SSM_BLOCKS:
                project_in(k + 1)
            for c in range(chunks_per_block):
                scan(k * chunks_per_block + c)
            ys.append(project_out(k))
    y = jnp.concatenate(ys, axis=-1) + d_ref[...] * u
    z = jax.nn.gelu(y).astype(BF16)
    vg = _dot(z, wglu_ref[...])
    out = vg[:, :D_MODEL] * jax.nn.sigmoid(vg[:, D_MODEL:])
    if n_time == 1:
        o_ref[...] = x_ref[...] + out
    else:
        for j, cols in enumerate(lane_tiles):
            tm_ref[j] = out[:, cols]
        for b in range(n_batch):
            rows = pl.ds(b, n_time, stride=n_batch)
            out_b = jnp.concatenate([tm_ref[j, rows, :] for j in range(len(lane_tiles))], axis=1)
            o_ref[b] = x_ref[b] + out_b

    @pl.when(step == pl.num_programs(0) - 1)
    def _():
        hf_re_ref[...] = st_re_ref[...]
        hf_im_ref[...] = st_im_ref[...]


def _ssm_mixer(x, g, wb, ab_re, ab_im, wc_re, wc_im, d_skip, w_glu, h0_re, h0_im,
               layer, j, n_batch, n_time_total):
    n_time = min(SSM_TIME_TILE, n_time_total)
    tile = n_time * n_batch
    if n_time_total == 1:
        x_in = x
        row_spec = _const_spec((n_batch, D_MODEL))
    else:
        x_in = x.reshape(n_batch, n_time_total, D_MODEL)
        row_spec = pl.BlockSpec((n_batch, n_time, D_MODEL), lambda i: (0, i, 0))
    state_spec = _const_spec((n_batch, SSM_LANES))
    state_shape = jax.ShapeDtypeStruct((n_batch, SSM_LANES), F32)
    blk = SSM_GROUP_BLOCK * STATE
    cblk = SSM_GROUP_BLOCK * GROUP
    out, h_re, h_im = pl.pallas_call(
        functools.partial(_ssm_kernel, n_batch=n_batch, n_time=n_time),
        out_shape=(jax.ShapeDtypeStruct(x_in.shape, F32), state_shape, state_shape),
        grid=(n_time_total // n_time,),
        in_specs=[row_spec,
                  _layer_spec((1, D_MODEL), layer),
                  _layer_spec((SSM_BLOCKS, cblk, 2 * blk), j),
                  _layer_spec((1, SSM_LANES), j),
                  _layer_spec((1, SSM_LANES), j),
                  _layer_spec((SSM_BLOCKS, blk, cblk), j),
                  _layer_spec((SSM_BLOCKS, blk, cblk), j),
                  _layer_spec((1, D_MODEL), j),
                  _layer_spec((D_MODEL, 2 * D_MODEL), j),
                  state_spec, state_spec],
        out_specs=(row_spec, state_spec, state_spec),
        scratch_shapes=[pltpu.VMEM((tile, SSM_LANES), F32), pltpu.VMEM((tile, SSM_LANES), F32),
                        pltpu.VMEM((n_batch, SSM_LANES), F32), pltpu.VMEM((n_batch, SSM_LANES), F32),
                        pltpu.VMEM((D_MODEL // V7X_LANES, tile, V7X_LANES), F32),
                        pltpu.VMEM((V7X_SUBLANES, SSM_LANES), F32),
                        pltpu.VMEM((V7X_SUBLANES, SSM_LANES), F32)],
        compiler_params=_params("arbitrary"),
        name="ssm_mixer",
    )(x_in, g, wb, ab_re, ab_im, wc_re, wc_im, d_skip, w_glu, h0_re, h0_im)
    return out.reshape(x.shape), h_re, h_im


def _block_diag(w, rows_inner, cols_inner):
    n_layers = w.shape[0]
    rows, cols = SSM_GROUP_BLOCK * rows_inner, SSM_GROUP_BLOCK * cols_inner
    tiled = jnp.tile(w.reshape(n_layers, SSM_BLOCKS, rows, cols_inner), (1, 1, 1, SSM_GROUP_BLOCK))
    row_group = lax.broadcasted_iota(jnp.int32, (rows, cols), 0) // rows_inner
    col_group = lax.broadcasted_iota(jnp.int32, (rows, cols), 1) // cols_inner
    return jnp.where(row_group == col_group, tiled, jnp.zeros_like(tiled))


def _mem_kv_kernel(m_ref, g_ref, wk_ref, wv_ref, k_ref, v_ref, kb_ref, vb_ref):
    m = _rmsnorm(m_ref[...], g_ref[...]).astype(BF16)
    k = _dot(m, wk_ref[...])
    v = _dot(m, wv_ref[...])
    kb_ref[...] = k.astype(BF16)
    vb_ref[...] = v.astype(BF16)
    for s in range(MEM_KV_SEQS):
        rows = slice(s * N_MEM, (s + 1) * N_MEM)
        for h in range(X_HEADS):
            cols = slice(h * X_HEAD_DIM, (h + 1) * X_HEAD_DIM)
            k_ref[s, :, h, :] = k[rows, cols]
            v_ref[s, :, h, :] = v[rows, cols]


def _mem_kv(mem, g, wk, wv):
    n_seq = mem.shape[0]
    tm = MEM_KV_SEQS * N_MEM
    out = jax.ShapeDtypeStruct((DEPTH, n_seq, N_MEM, X_HEADS, X_HEAD_DIM), F32)
    out_b = jax.ShapeDtypeStruct((DEPTH, n_seq * N_MEM, D_MODEL), BF16)
    w_spec = pl.BlockSpec((None, D_MODEL, D_MODEL), lambda l, i: (l, 0, 0))
    o_spec = pl.BlockSpec((None, MEM_KV_SEQS, N_MEM, X_HEADS, X_HEAD_DIM), lambda l, i: (l, i, 0, 0, 0))
    ob_spec = pl.BlockSpec((None, tm, D_MODEL), lambda l, i: (l, i, 0))
    return pl.pallas_call(
        _mem_kv_kernel,
        out_shape=(out, out, out_b, out_b),
        grid=(DEPTH, n_seq // MEM_KV_SEQS),
        in_specs=[pl.BlockSpec((tm, D_MODEL), lambda l, i: (i, 0)),
                  pl.BlockSpec((None, 1, D_MODEL), lambda l, i: (l, 0, 0)),
                  w_spec, w_spec],
        out_specs=(o_spec, o_spec, ob_spec, ob_spec),
        compiler_params=_params("parallel", "parallel"),
        name="mem_kv",
    )(mem.reshape(n_seq * N_MEM, D_MODEL), g, wk, wv)


def _softmax_rows(s):
    e = jnp.exp(s - jnp.max(s, axis=-1, keepdims=True))
    return e / jnp.sum(e, axis=-1, keepdims=True)


def _attn_kernel(x_ref, xs_ref, g_ref, wq_ref, k_ref, v_ref, ck_ref, cv_ref, wo_ref, o_ref, os_ref,
                 oh_ref, qs_ref, ohs_ref, *, nb):
    b, t = pl.program_id(0), pl.program_id(1)
    step = b * pl.num_programs(1) + t
    last = pl.num_programs(0) * pl.num_programs(1) - 1
    scale = X_HEAD_DIM ** -0.5
    head_cols = [slice(h * X_HEAD_DIM, (h + 1) * X_HEAD_DIM) for h in range(X_HEADS)]

    @pl.when(step == 0)
    def _():
        hs = _rmsnorm(xs_ref[...], g_ref[...]).astype(BF16)
        qs_ref[...] = _dot(hs, wq_ref[...])
        ohs_ref[...] = jnp.zeros_like(ohs_ref)

    x = x_ref[...]
    h = _rmsnorm(x, g_ref[...]).astype(BF16)
    q = _dot(h, wq_ref[...])
    qb = q.astype(BF16)
    s_all = jnp.concatenate(
        [lax.dot_general(qb[:, cols], k_ref[:, cols], (((1,), (1,)), ((), ())),
                         preferred_element_type=F32) for cols in head_cols], axis=0) * scale
    p_all = _softmax_rows(s_all).astype(BF16)
    tm = x_ref.shape[0]
    for i, cols in enumerate(head_cols):
        oh_ref[:, cols] = _dot(p_all[i * tm:(i + 1) * tm, :], v_ref[:, cols]).astype(BF16)
    o_ref[...] = x + _dot(oh_ref[...], wo_ref[...])

    parts = V7X_SUBLANES // nb
    sub = step % parts
    row0 = pl.multiple_of((step // parts) * V7X_SUBLANES, V7X_SUBLANES)
    q_group = qs_ref[pl.ds(row0, V7X_SUBLANES), :]
    q_rows = q_group[:nb, :]
    for c in range(1, parts):
        q_rows = jnp.where(sub == c, q_group[c * nb:(c + 1) * nb, :], q_rows)
    n_rows = N_MEM * X_HEADS
    head_of_lane = lax.broadcasted_iota(jnp.int32, (V7X_SUBLANES, n_rows), 1) % X_HEADS
    head_of_row = lax.broadcasted_iota(jnp.int32, (V7X_SUBLANES, n_rows), 0) % X_HEADS
    own_head = head_of_lane == head_of_row
    scores = []
    for r in range(nb):
        qr = q_rows[r:r + 1, :]
        qh = jnp.concatenate([qr[:, c] for c in head_cols] * (V7X_SUBLANES // X_HEADS), axis=0)
        k2 = ck_ref[r].reshape(n_rows, X_HEAD_DIM).astype(BF16)
        scores.append(lax.dot_general(qh.astype(BF16), k2, (((1,), (1,)), ((), ())),
                                      preferred_element_type=F32))
    sc = jnp.concatenate(scores, axis=0) * scale
    own = jnp.concatenate([own_head] * nb, axis=0)
    p_all = _softmax_rows(jnp.where(own, sc, -jnp.inf)).astype(BF16)
    outs = []
    for r in range(nb):
        v2 = cv_ref[r].reshape(n_rows, X_HEAD_DIM).astype(BF16)
        pv = _dot(p_all[r * V7X_SUBLANES:(r + 1) * V7X_SUBLANES, :], v2)
        outs.append(jnp.concatenate([pv[hd:hd + 1, :] for hd in range(X_HEADS)], axis=1))
    o_group = jnp.concatenate(outs * parts, axis=0)
    part_of_row = lax.broadcasted_iota(jnp.int32, (V7X_SUBLANES, D_MODEL), 0) // nb
    rows = pl.ds(row0, V7X_SUBLANES)
    ohs_ref[rows, :] = jnp.where(part_of_row == sub, o_group, ohs_ref[rows, :])

    @pl.when(step == last)
    def _():
        os_ref[...] = xs_ref[...] + _dot(ohs_ref[...].astype(BF16), wo_ref[...])


def _attn(x, xs, g, wq, k, v, cache_k, cache_v, wo, layer, n_seq, seq_len):
    tm = ROW_TILE
    n_t = seq_len // tm
    nb, rem = divmod(xs.shape[0], n_seq * n_t)
    assert rem == 0 and V7X_SUBLANES % nb == 0
    row_spec = pl.BlockSpec((tm, D_MODEL), lambda b, i: (b * n_t + i, 0))
    kv_spec = pl.BlockSpec((None, N_MEM, D_MODEL), lambda b, i: (layer, b, 0))
    cache_spec = pl.BlockSpec((None, nb, N_MEM, X_HEADS, X_HEAD_DIM),
                              lambda b, i: (layer, b * n_t + i, 0, 0, 0))
    return pl.pallas_call(
        functools.partial(_attn_kernel, nb=nb),
        out_shape=(jax.ShapeDtypeStruct(x.shape, F32), jax.ShapeDtypeStruct(xs.shape, F32)),
        grid=(n_seq, n_t),
        in_specs=[row_spec,
                  _const_spec(xs.shape),
                  _layer_spec((1, D_MODEL), layer),
                  _layer_spec((D_MODEL, D_MODEL), layer),
                  kv_spec, kv_spec, cache_spec, cache_spec,
                  _layer_spec((D_MODEL, D_MODEL), layer)],
        out_specs=(row_spec, _const_spec(xs.shape)),
        scratch_shapes=[pltpu.VMEM((tm, D_MODEL), BF16),
                        pltpu.VMEM(xs.shape, F32), pltpu.VMEM(xs.shape, F32)],
        compiler_params=_params("arbitrary", "arbitrary"),
        name="attn",
    )(x, xs, g, wq, k, v, cache_k, cache_v, wo)


def _trunk(x, xs, n_seq, seq_len, mem_k, mem_v, cache_k, cache_v, state_re, state_im, w):
    n_samples = xs.shape[0]
    zero_state = jnp.zeros((n_seq, SSM_LANES), F32)
    p_re, p_im, s_re, s_im, s_v = [], [], [], [], []
    for i in range(DEPTH):
        j = i // 2
        x, xs = _ffn(x, xs, w['norm_ffn1'], w['ffn1_wg'], w['ffn1_wu'], w['ffn1_wd'], w['norm_final'],
                     i, False)
        if i % 2 == 0:
            x, xs, v_rows = _mixer_a(x, xs, w['norm_mix'], w['a_w_in'], w['a_ln_g'], w['a_ln_b'],
                                     w['a_w_s'], w['a_bias_full'], w['a_scale0'], w['a_bias0'],
                                     w['a_w_out'], i, j)
            s_v.append(v_rows)
        else:
            ssm_w = (w['norm_mix'], w['b_wb'], w['b_ab_re'], w['b_ab_im'], w['b_wc_re'], w['b_wc_im'],
                     w['b_d'], w['b_w_glu'])
            x, hr, hi = _ssm_mixer(x, *ssm_w, zero_state, zero_state, i, j, n_seq, seq_len)
            p_re.append(hr)
            p_im.append(hi)
            xs, hr, hi = _ssm_mixer(xs, *ssm_w, state_re[j], state_im[j], i, j, n_samples, 1)
            s_re.append(hr)
            s_im.append(hi)
        x, xs = _attn(x, xs, w['norm_x'], w['x_wq'], mem_k, mem_v, cache_k, cache_v, w['x_wo'],
                      i, n_seq, seq_len)
        x, xs = _ffn(x, xs, w['norm_ffn2'], w['ffn2_wg'], w['ffn2_wu'], w['ffn2_wd'], w['norm_final'],
                     i, i == DEPTH - 1)
    return x, xs, p_re, p_im, s_re, s_im, s_v


def kernel(x_prompt, x_sample, cache_mem_k, cache_mem_v, state_ssm_re, state_ssm_im, mem_prompt,
           norm_ffn1, ffn1_wg, ffn1_wu, ffn1_wd, norm_mix,
           a_w_in, a_ln_g, a_ln_b, a_w_s, a_b_s, a_w_out,
           b_a_re, b_a_im, b_log_dt, b_b_re, b_b_im, b_c_re, b_c_im, b_d, b_w_glu,
           norm_x, norm_mem, x_wq, x_wk, x_wv, x_wo,
           norm_ffn2, ffn2_wg, ffn2_wu, ffn2_wd, norm_final):
    batch, seq, _ = x_prompt.shape
    dec_batch, dec_seq, _ = x_sample.shape
    assert dec_seq == 1 and seq % ROW_TILE == 0 and ROW_TILE % CHUNK == 0
    n_a = a_w_in.shape[0]
    n_b = b_a_re.shape[0]

    def vec(p):
        return p.reshape(p.shape[0], 1, p.shape[-1])

    ab_re, ab_im, bb_re, bb_im = _ssm_discretize(
        b_a_re, b_a_im, b_log_dt, b_b_re.transpose(0, 1, 3, 2), b_b_im.transpose(0, 1, 3, 2))
    wb = jnp.concatenate([_block_diag(bb_re.astype(BF16), GROUP, STATE),
                          _block_diag(bb_im.astype(BF16), GROUP, STATE)], axis=-1)
    w = dict(
        norm_ffn1=vec(norm_ffn1), norm_ffn2=vec(norm_ffn2), norm_mix=vec(norm_mix), norm_x=vec(norm_x),
        norm_final=norm_final.reshape(1, D_MODEL),
        ffn1_wg=ffn1_wg, ffn1_wu=ffn1_wu, ffn1_wd=ffn1_wd,
        ffn2_wg=ffn2_wg, ffn2_wu=ffn2_wu, ffn2_wd=ffn2_wd,
        a_w_in=a_w_in, a_w_out=a_w_out,
        a_ln_g=vec(a_ln_g), a_ln_b=vec(a_ln_b), a_w_s=a_w_s,
        a_bias_full=jnp.repeat(jnp.swapaxes(a_b_s, 1, 2), A_HEAD_DIM, axis=-1),
        a_scale0=jnp.repeat(a_w_s[:, :, 0, 0], A_HEAD_DIM, axis=-1).reshape(n_a, 1, D_A),
        a_bias0=jnp.repeat(a_b_s[:, :, 0], A_HEAD_DIM, axis=-1).reshape(n_a, 1, D_A),
        b_wb=wb,
        b_ab_re=ab_re.reshape(n_b, 1, SSM_LANES), b_ab_im=ab_im.reshape(n_b, 1, SSM_LANES),
        b_wc_re=_block_diag(b_c_re.transpose(0, 1, 3, 2).astype(BF16), STATE, GROUP),
        b_wc_im=_block_diag(b_c_im.transpose(0, 1, 3, 2).astype(BF16), STATE, GROUP),
        b_d=b_d.reshape(n_b, 1, D_MODEL), b_w_glu=b_w_glu,
        x_wq=x_wq, x_wo=x_wo,
    )

    mem_k, mem_v, mem_kb, mem_vb = _mem_kv(mem_prompt, vec(norm_mem), x_wk, x_wv)
    y_prompt, y_sample, p_re, p_im, s_re, s_im, s_v = _trunk(
        x_prompt.reshape(batch * seq, D_MODEL), x_sample.reshape(dec_batch, D_MODEL), batch, seq,
        mem_kb, mem_vb, cache_mem_k, cache_mem_v,
        state_ssm_re.reshape(n_b, dec_batch, SSM_LANES),
        state_ssm_im.reshape(n_b, dec_batch, SSM_LANES), w)

    def states(parts, n):
        return jnp.stack(parts).reshape(n_b, n, N_GROUPS, STATE)

    return (y_prompt.reshape(batch, seq, D_MODEL),
            y_sample.reshape(dec_batch, 1, D_MODEL),
            mem_k, mem_v,
            states(p_re, batch), states(p_im, batch),
            states(s_re, dec_batch), states(s_im, dec_batch),
            jnp.stack(s_v).reshape(n_a, dec_batch, 1, D_A))
```

```python
import functools

import jax
import jax.numpy as jnp
from jax import lax
from jax.experimental import pallas as pl
from jax.experimental.pallas import tpu as pltpu

F32 = jnp.float32
BF16 = jnp.bfloat16

D_MODEL = 1024
DEPTH = 4
CHUNK = 128
D_A = 2 * D_MODEL
A_HEADS = 8
A_HEAD_DIM = D_A // A_HEADS
GROUP = 16
N_GROUPS = D_MODEL // GROUP
STATE = 64
SSM_LANES = N_GROUPS * STATE
N_MEM = 256
X_HEADS = 4
X_HEAD_DIM = D_MODEL // X_HEADS
D_FF = 2816
EPS = 1e-6

V7X_SUBLANES = 8
V7X_LANES = 128
V7X_VMEM_LIMIT_BYTES = 56 * 1024 * 1024

ROW_TILE = 512
FFN_FF_CHUNK = 768
SSM_TIME_TILE = 64
SSM_GROUP_BLOCK = 16
SSM_BLOCKS = N_GROUPS // SSM_GROUP_BLOCK
SCAN_LANES = 512
MEM_KV_SEQS = 2


def _params(*semantics):
    return pltpu.CompilerParams(dimension_semantics=semantics,
                                vmem_limit_bytes=V7X_VMEM_LIMIT_BYTES)


def _const_spec(shape):
    nd = len(shape)
    return pl.BlockSpec(shape, lambda *_: (0,) * nd, pipeline_mode=pl.Buffered(1))


def _layer_spec(shape, layer):
    nd = len(shape)
    return pl.BlockSpec((None,) + tuple(shape), lambda *_: (layer,) + (0,) * nd,
                        pipeline_mode=pl.Buffered(1))


def _rmsnorm(x, g):
    return x * lax.rsqrt(jnp.mean(x * x, axis=-1, keepdims=True) + EPS) * g


def _dot(a, b):
    return lax.dot_general(a, b, (((1,), (0,)), ((), ())), preferred_element_type=F32)


def _ffn_kernel(x_ref, xs_ref, g_ref, wg_hbm, wu_hbm, wd_hbm, gf_ref, o_ref, os_ref,
                wg_ref, wu_ref, wd_ref, sem, *, layer, final_norm):
    step = pl.program_id(0)
    chunks = [(c0, min(FFN_FF_CHUNK, D_FF - c0)) for c0 in range(0, D_FF, FFN_FF_CHUNK)]

    def copies(c):
        c0, w = chunks[c]
        return (pltpu.make_async_copy(wg_hbm.at[layer, :, pl.ds(c0, w)], wg_ref.at[:, pl.ds(c0, w)],
                                      sem.at[0, c]),
                pltpu.make_async_copy(wu_hbm.at[layer, :, pl.ds(c0, w)], wu_ref.at[:, pl.ds(c0, w)],
                                      sem.at[1, c]),
                pltpu.make_async_copy(wd_hbm.at[layer, pl.ds(c0, w), :], wd_ref.at[pl.ds(c0, w), :],
                                      sem.at[2, c]))

    def wait_chunk(c):
        for cp in copies(c):
            cp.wait()

    def ffn_rows(x, before_chunk=None):
        h = _rmsnorm(x, g_ref[...]).astype(BF16)
        down = None
        for c, (c0, w) in enumerate(chunks):
            if before_chunk is not None:
                before_chunk(c)
            cols = slice(c0, c0 + w)
            gate = _dot(h, wg_ref[:, cols])
            up = _dot(h, wu_ref[:, cols])
            act = (jax.nn.silu(gate) * up).astype(BF16)
            part = _dot(act, wd_ref[cols, :])
            down = part if down is None else down + part
        y = x + 0.5 * down
        return _rmsnorm(y, gf_ref[...]) if final_norm else y

    @pl.when(step == 0)
    def _():
        for c in range(len(chunks)):
            for cp in copies(c):
                cp.start()
        o_ref[...] = ffn_rows(x_ref[...], wait_chunk)

    @pl.when(step != 0)
    def _():
        o_ref[...] = ffn_rows(x_ref[...])

    @pl.when(step == pl.num_programs(0) - 1)
    def _():
        os_ref[...] = ffn_rows(xs_ref[...])


def _ffn(x, xs, g, wg, wu, wd, g_final, layer, final_norm):
    rows = x.shape[0]
    tm = ROW_TILE
    assert rows // tm > 1
    row_spec = pl.BlockSpec((tm, D_MODEL), lambda i: (i, 0))
    hbm_spec = pl.BlockSpec(memory_space=pl.ANY)
    n_chunks = pl.cdiv(D_FF, FFN_FF_CHUNK)
    return pl.pallas_call(
        functools.partial(_ffn_kernel, layer=layer, final_norm=final_norm),
        out_shape=(jax.ShapeDtypeStruct(x.shape, F32), jax.ShapeDtypeStruct(xs.shape, F32)),
        grid=(rows // tm,),
        in_specs=[row_spec,
                  _const_spec(xs.shape),
                  _layer_spec((1, D_MODEL), layer),
                  hbm_spec, hbm_spec, hbm_spec,
                  _const_spec((1, D_MODEL))],
        out_specs=(row_spec, _const_spec(xs.shape)),
        scratch_shapes=[pltpu.VMEM((D_MODEL, D_FF), F32), pltpu.VMEM((D_MODEL, D_FF), F32),
                        pltpu.VMEM((D_FF, D_MODEL), F32),
                        pltpu.SemaphoreType.DMA((3, n_chunks))],
        compiler_params=_params("arbitrary"),
        name="ffn",
    )(x, xs, g, wg, wu, wd, g_final)


def _gated_unit_inputs(x_ref, g_ref, w_in_ref, ln_g_ref, ln_b_ref):
    h = _rmsnorm(x_ref[...], g_ref[...]).astype(BF16)
    uv = jax.nn.gelu(_dot(h, w_in_ref[...]))
    u = uv[:, :D_A]
    v = uv[:, D_A:]
    mu = jnp.mean(v, axis=-1, keepdims=True)
    vc = v - mu
    var = jnp.mean(vc * vc, axis=-1, keepdims=True)
    v = vc * lax.rsqrt(var + EPS) * ln_g_ref[...] + ln_b_ref[...]
    return u, v


def _mixer_a_kernel(x_ref, xs_ref, g_ref, w_in_ref, ln_g_ref, ln_b_ref, ws_ref, bias_ref,
                    scale0_ref, bias0_ref, w_out_ref, o_ref, os_ref, vs_ref, gated_ref):
    u, v = _gated_unit_inputs(x_ref, g_ref, w_in_ref, ln_g_ref, ln_b_ref)
    vb = v.astype(BF16)
    t_idx = lax.broadcasted_iota(jnp.int32, (CHUNK, CHUNK), 0)
    s_idx = lax.broadcasted_iota(jnp.int32, (CHUNK, CHUNK), 1)
    causal = t_idx >= s_idx
    for head in range(A_HEADS):
        cols = slice(head * A_HEAD_DIM, (head + 1) * A_HEAD_DIM)
        ws = jnp.where(causal, ws_ref[head], 0.0).astype(BF16)
        bias = bias_ref[:, cols]
        for c in range(x_ref.shape[0] // CHUNK):
            rows = slice(c * CHUNK, (c + 1) * CHUNK)
            mixed = _dot(ws, vb[rows, cols]) + bias
            gated_ref[rows, cols] = (u[rows, cols] * mixed).astype(BF16)
    o_ref[...] = x_ref[...] + _dot(gated_ref[...], w_out_ref[...])

    @pl.when(pl.program_id(0) == pl.num_programs(0) - 1)
    def _():
        us, vs = _gated_unit_inputs(xs_ref, g_ref, w_in_ref, ln_g_ref, ln_b_ref)
        vs_ref[...] = vs
        mixed = vs * scale0_ref[...] + bias0_ref[...]
        os_ref[...] = xs_ref[...] + _dot((us * mixed).astype(BF16), w_out_ref[...])


def _mixer_a(x, xs, g, w_in, ln_g, ln_b, w_s, bias_full, scale0, bias0, w_out, layer, j):
    rows = x.shape[0]
    tm = ROW_TILE
    row_spec = pl.BlockSpec((tm, D_MODEL), lambda i: (i, 0))
    vs_shape = (xs.shape[0], D_A)
    return pl.pallas_call(
        _mixer_a_kernel,
        out_shape=(jax.ShapeDtypeStruct(x.shape, F32), jax.ShapeDtypeStruct(xs.shape, F32),
                   jax.ShapeDtypeStruct(vs_shape, F32)),
        grid=(rows // tm,),
        in_specs=[row_spec,
                  _const_spec(xs.shape),
                  _layer_spec((1, D_MODEL), layer),
                  _layer_spec((D_MODEL, 2 * D_A), j),
                  _layer_spec((1, D_A), j),
                  _layer_spec((1, D_A), j),
                  _layer_spec((A_HEADS, CHUNK, CHUNK), j),
                  _layer_spec((CHUNK, D_A), j),
                  _layer_spec((1, D_A), j),
                  _layer_spec((1, D_A), j),
                  _layer_spec((D_A, D_MODEL), j)],
        out_specs=(row_spec, _const_spec(xs.shape), _const_spec(vs_shape)),
        scratch_shapes=[pltpu.VMEM((tm, D_A), BF16)],
        compiler_params=_params("arbitrary"),
        name="mixer_a",
    )(x, xs, g, w_in, ln_g, ln_b, w_s, bias_full, scale0, bias0, w_out)


def _ssm_discretize_kernel(a_re_ref, a_im_ref, log_dt_ref, b_re_ref, b_im_ref,
                           ab_re_ref, ab_im_ref, bb_re_ref, bb_im_ref):
    a_re = a_re_ref[...]
    a_im = a_im_ref[...]
    dt = jnp.exp(log_dt_ref[...])
    mag = jnp.exp(a_re * dt)
    ab_re = mag * jnp.cos(a_im * dt)
    ab_im = mag * jnp.sin(a_im * dt)
    den = a_re * a_re + a_im * a_im
    nr = ab_re - 1.0
    ni = ab_im
    f_re = (nr * a_re + ni * a_im) / den
    f_im = (ni * a_re - nr * a_im) / den
    b_re = b_re_ref[...]
    b_im = b_im_ref[...]
    ab_re_ref[...] = ab_re
    ab_im_ref[...] = ab_im
    bb_re_ref[...] = f_re * b_re - f_im * b_im
    bb_im_ref[...] = f_re * b_im + f_im * b_re


def _ssm_discretize(a_re, a_im, log_dt, b_re_t, b_im_t):
    n_layers = a_re.shape[0]
    ab_shape = jax.ShapeDtypeStruct((n_layers, N_GROUPS, 1, STATE), F32)
    bb_shape = jax.ShapeDtypeStruct((n_layers, N_GROUPS, GROUP, STATE), F32)
    return pl.pallas_call(
        _ssm_discretize_kernel,
        out_shape=(ab_shape, ab_shape, bb_shape, bb_shape),
        name="ssm_discretize",
    )(a_re.reshape(ab_shape.shape), a_im.reshape(ab_shape.shape),
      log_dt.reshape(n_layers, N_GROUPS, 1, 1), b_re_t, b_im_t)


def _ssm_kernel(x_ref, g_ref, wb_re_ref, wb_im_ref, ab_re_ref, ab_im_ref, wc_re_ref, wc_im_ref, d_ref,
                wglu_ref, h0_re_ref, h0_im_ref, o_ref, hf_re_ref, hf_im_ref,
                s_re_ref, s_im_ref, st_re_ref, st_im_ref, tm_ref, abb_re_ref, abb_im_ref,
                *, n_batch, n_time):
    step = pl.program_id(0)
    blk = SSM_GROUP_BLOCK * STATE
    cblk = SSM_GROUP_BLOCK * GROUP
    lane_tiles = [slice(j * V7X_LANES, (j + 1) * V7X_LANES) for j in range(D_MODEL // V7X_LANES)]

    @pl.when(step == 0)
    def _():
        st_re_ref[...] = h0_re_ref[...]
        st_im_ref[...] = h0_im_ref[...]
        abb_re_ref[...] = jnp.broadcast_to(ab_re_ref[...], abb_re_ref.shape)
        abb_im_ref[...] = jnp.broadcast_to(ab_im_ref[...], abb_im_ref.shape)

    if n_time == 1:
        u = _rmsnorm(x_ref[...], g_ref[...])
    else:
        for b in range(n_batch):
            ub_rows = _rmsnorm(x_ref[b], g_ref[...])
            for j, cols in enumerate(lane_tiles):
                tm_ref[j, pl.ds(b, n_time, stride=n_batch), :] = ub_rows[:, cols]
        u = jnp.concatenate([tm_ref[j] for j in range(len(lane_tiles))], axis=1)
    ub = u.astype(BF16)

    def project_in(k):
        uk = ub[:, k * cblk:(k + 1) * cblk]
        s_re_ref[:, k * blk:(k + 1) * blk] = _dot(uk, wb_re_ref[k])
        s_im_ref[:, k * blk:(k + 1) * blk] = _dot(uk, wb_im_ref[k])

    def scan(c):
        lanes = slice(c * SCAN_LANES, (c + 1) * SCAN_LANES)
        ar = abb_re_ref[:, lanes]
        ai = abb_im_ref[:, lanes]
        for s in range(n_batch // V7X_SUBLANES):
            rows0 = slice(s * V7X_SUBLANES, (s + 1) * V7X_SUBLANES)
            hr = st_re_ref[rows0, lanes]
            hi = st_im_ref[rows0, lanes]
            for t in range(n_time):
                rows = slice(t * n_batch + s * V7X_SUBLANES, t * n_batch + (s + 1) * V7X_SUBLANES)
                hr, hi = (ar * hr - ai * hi + s_re_ref[rows, lanes],
                          ar * hi + ai * hr + s_im_ref[rows, lanes])
                s_re_ref[rows, lanes] = hr
                s_im_ref[rows, lanes] = hi
            st_re_ref[rows0, lanes] = hr
            st_im_ref[rows0, lanes] = hi

    def project_out(k):
        hr = s_re_ref[:, k * blk:(k + 1) * blk].astype(BF16)
        hi = s_im_ref[:, k * blk:(k + 1) * blk].astype(BF16)
        return _dot(hr, wc_re_ref[k]) - _dot(hi, wc_im_ref[k])

    ys = []
    if n_time == 1:
        for k in range(SSM_BLOCKS):
            project_in(k)
        ar = ab_re_ref[...]
        ai = ab_im_ref[...]
        hr0 = st_re_ref[...]
        hi0 = st_im_ref[...]
        hr = ar * hr0 - ai * hi0 + s_re_ref[...]
        hi = ar * hi0 + ai * hr0 + s_im_ref[...]
        s_re_ref[...] = hr
        s_im_ref[...] = hi
        st_re_ref[...] = hr
        st_im_ref[...] = hi
        for k in range(SSM_BLOCKS):
            ys.append(project_out(k))
    else:
        chunks_per_block = blk // SCAN_LANES
        project_in(0)
        for k in range(SSM_BLOCKS):
            if k + 1 < SSM_BLOCKS:
                project_in(k + 1)
            for c in range(chunks_per_block):
                scan(k * chunks_per_block + c)
            ys.append(project_out(k))
    y = jnp.concatenate(ys, axis=-1) + d_ref[...] * u
    z = jax.nn.gelu(y).astype(BF16)
    vg = _dot(z, wglu_ref[...])
    out = vg[:, :D_MODEL] * jax.nn.sigmoid(vg[:, D_MODEL:])
    if n_time == 1:
        o_ref[...] = x_ref[...] + out
    else:
        for j, cols in enumerate(lane_tiles):
            tm_ref[j] = out[:, cols]
        for b in range(n_batch):
            rows = pl.ds(b, n_time, stride=n_batch)
            out_b = jnp.concatenate([tm_ref[j, rows, :] for j in range(len(lane_tiles))], axis=1)
            o_ref[b] = x_ref[b] + out_b

    @pl.when(step == pl.num_programs(0) - 1)
    def _():
        hf_re_ref[...] = st_re_ref[...]
        hf_im_ref[...] = st_im_ref[...]


def _ssm_mixer(x, g, wb_re, wb_im, ab_re, ab_im, wc_re, wc_im, d_skip, w_glu, h0_re, h0_im,
               layer, j, n_batch, n_time_total):
    n_time = min(SSM_TIME_TILE, n_time_total)
    tile = n_time * n_batch
    if n_time_total == 1:
        x_in = x
        row_spec = _const_spec((n_batch, D_MODEL))
    else:
        x_in = x.reshape(n_batch, n_time_total, D_MODEL)
        row_spec = pl.BlockSpec((n_batch, n_time, D_MODEL), lambda i: (0, i, 0))
    state_spec = _const_spec((n_batch, SSM_LANES))
    state_shape = jax.ShapeDtypeStruct((n_batch, SSM_LANES), F32)
    blk = SSM_GROUP_BLOCK * STATE
    cblk = SSM_GROUP_BLOCK * GROUP
    out, h_re, h_im = pl.pallas_call(
        functools.partial(_ssm_kernel, n_batch=n_batch, n_time=n_time),
        out_shape=(jax.ShapeDtypeStruct(x_in.shape, F32), state_shape, state_shape),
        grid=(n_time_total // n_time,),
        in_specs=[row_spec,
                  _layer_spec((1, D_MODEL), layer),
                  _layer_spec((SSM_BLOCKS, cblk, blk), j),
                  _layer_spec((SSM_BLOCKS, cblk, blk), j),
                  _layer_spec((1, SSM_LANES), j),
                  _layer_spec((1, SSM_LANES), j),
                  _layer_spec((SSM_BLOCKS, blk, cblk), j),
                  _layer_spec((SSM_BLOCKS, blk, cblk), j),
                  _layer_spec((1, D_MODEL), j),
                  _layer_spec((D_MODEL, 2 * D_MODEL), j),
                  state_spec, state_spec],
        out_specs=(row_spec, state_spec, state_spec),
        scratch_shapes=[pltpu.VMEM((tile, SSM_LANES), F32), pltpu.VMEM((tile, SSM_LANES), F32),
                        pltpu.VMEM((n_batch, SSM_LANES), F32), pltpu.VMEM((n_batch, SSM_LANES), F32),
                        pltpu.VMEM((D_MODEL // V7X_LANES, tile, V7X_LANES), F32),
                        pltpu.VMEM((V7X_SUBLANES, SSM_LANES), F32),
                        pltpu.VMEM((V7X_SUBLANES, SSM_LANES), F32)],
        compiler_params=_params("arbitrary"),
        name="ssm_mixer",
    )(x_in, g, wb_re, wb_im, ab_re, ab_im, wc_re, wc_im, d_skip, w_glu, h0_re, h0_im)
    return out.reshape(x.shape), h_re, h_im


def _block_diag(w, rows_inner, cols_inner):
    n_layers = w.shape[0]
    rows, cols = SSM_GROUP_BLOCK * rows_inner, SSM_GROUP_BLOCK * cols_inner
    tiled = jnp.tile(w.reshape(n_layers, SSM_BLOCKS, rows, cols_inner), (1, 1, 1, SSM_GROUP_BLOCK))
    row_group = lax.broadcasted_iota(jnp.int32, (rows, cols), 0) // rows_inner
    col_group = lax.broadcasted_iota(jnp.int32, (rows, cols), 1) // cols_inner
    return jnp.where(row_group == col_group, tiled, jnp.zeros_like(tiled))


def _mem_kv_kernel(m_ref, g_ref, wk_ref, wv_ref, k_ref, v_ref, kb_ref, vb_ref):
    m = _rmsnorm(m_ref[...], g_ref[...]).astype(BF16)
    k = _dot(m, wk_ref[...])
    v = _dot(m, wv_ref[...])
    kb_ref[...] = k.astype(BF16)
    vb_ref[...] = v.astype(BF16)
    for s in range(MEM_KV_SEQS):
        rows = slice(s * N_MEM, (s + 1) * N_MEM)
        for h in range(X_HEADS):
            cols = slice(h * X_HEAD_DIM, (h + 1) * X_HEAD_DIM)
            k_ref[s, :, h, :] = k[rows, cols]
            v_ref[s, :, h, :] = v[rows, cols]


def _mem_kv(mem, g, wk, wv):
    n_seq = mem.shape[0]
    tm = MEM_KV_SEQS * N_MEM
    out = jax.ShapeDtypeStruct((DEPTH, n_seq, N_MEM, X_HEADS, X_HEAD_DIM), F32)
    out_b = jax.ShapeDtypeStruct((DEPTH, n_seq * N_MEM, D_MODEL), BF16)
    w_spec = pl.BlockSpec((None, D_MODEL, D_MODEL), lambda l, i: (l, 0, 0))
    o_spec = pl.BlockSpec((None, MEM_KV_SEQS, N_MEM, X_HEADS, X_HEAD_DIM), lambda l, i: (l, i, 0, 0, 0))
    ob_spec = pl.BlockSpec((None, tm, D_MODEL), lambda l, i: (l, i, 0))
    return pl.pallas_call(
        _mem_kv_kernel,
        out_shape=(out, out, out_b, out_b),
        grid=(DEPTH, n_seq // MEM_KV_SEQS),
        in_specs=[pl.BlockSpec((tm, D_MODEL), lambda l, i: (i, 0)),
                  pl.BlockSpec((None, 1, D_MODEL), lambda l, i: (l, 0, 0)),
                  w_spec, w_spec],
        out_specs=(o_spec, o_spec, ob_spec, ob_spec),
        compiler_params=_params("parallel", "parallel"),
        name="mem_kv",
    )(mem.reshape(n_seq * N_MEM, D_MODEL), g, wk, wv)


def _softmax_rows(s):
    e = jnp.exp(s - jnp.max(s, axis=-1, keepdims=True))
    return e / jnp.sum(e, axis=-1, keepdims=True)


def _attn_kernel(x_ref, xs_ref, g_ref, wq_ref, k_ref, v_ref, ck_ref, cv_ref, wo_ref, o_ref, os_ref,
                 oh_ref, qs_ref, ohs_ref, *, nb):
    b, t = pl.program_id(0), pl.program_id(1)
    step = b * pl.num_programs(1) + t
    last = pl.num_programs(0) * pl.num_programs(1) - 1
    scale = X_HEAD_DIM ** -0.5
    head_cols = [slice(h * X_HEAD_DIM, (h + 1) * X_HEAD_DIM) for h in range(X_HEADS)]

    @pl.when(step == 0)
    def _():
        hs = _rmsnorm(xs_ref[...], g_ref[...]).astype(BF16)
        qs_ref[...] = _dot(hs, wq_ref[...])
        ohs_ref[...] = jnp.zeros_like(ohs_ref)

    x = x_ref[...]
    h = _rmsnorm(x, g_ref[...]).astype(BF16)
    q = _dot(h, wq_ref[...])
    qb = q.astype(BF16)
    s_all = jnp.concatenate(
        [lax.dot_general(qb[:, cols], k_ref[:, cols], (((1,), (1,)), ((), ())),
                         preferred_element_type=F32) for cols in head_cols], axis=0) * scale
    p_all = _softmax_rows(s_all).astype(BF16)
    tm = x_ref.shape[0]
    for i, cols in enumerate(head_cols):
        oh_ref[:, cols] = _dot(p_all[i * tm:(i + 1) * tm, :], v_ref[:, cols]).astype(BF16)
    o_ref[...] = x + _dot(oh_ref[...], wo_ref[...])

    parts = V7X_SUBLANES // nb
    sub = step % parts
    row0 = pl.multiple_of((step // parts) * V7X_SUBLANES, V7X_SUBLANES)
    q_group = qs_ref[pl.ds(row0, V7X_SUBLANES), :]
    q_rows = q_group[:nb, :]
    for c in range(1, parts):
        q_rows = jnp.where(sub == c, q_group[c * nb:(c + 1) * nb, :], q_rows)
    n_rows = N_MEM * X_HEADS
    head_of_lane = lax.broadcasted_iota(jnp.int32, (V7X_SUBLANES, n_rows), 1) % X_HEADS
    head_of_row = lax.broadcasted_iota(jnp.int32, (V7X_SUBLANES, n_rows), 0) % X_HEADS
    own_head = head_of_lane == head_of_row
    scores = []
    for r in range(nb):
        qr = q_rows[r:r + 1, :]
        qh = jnp.concatenate([qr[:, c] for c in head_cols] * (V7X_SUBLANES // X_HEADS), axis=0)
        k2 = ck_ref[r].reshape(n_rows, X_HEAD_DIM).astype(BF16)
        scores.append(lax.dot_general(qh.astype(BF16), k2, (((1,), (1,)), ((), ())),
                                      preferred_element_type=F32))
    sc = jnp.concatenate(scores, axis=0) * scale
    own = jnp.concatenate([own_head] * nb, axis=0)
    p_all = _softmax_rows(jnp.where(own, sc, -jnp.inf)).astype(BF16)
    outs = []
    for r in range(nb):
        v2 = cv_ref[r].reshape(n_rows, X_HEAD_DIM).astype(BF16)
        pv = _dot(p_all[r * V7X_SUBLANES:(r + 1) * V7X_SUBLANES, :], v2)
        outs.append(jnp.concatenate([pv[hd:hd + 1, :] for hd in range(X_HEADS)], axis=1))
    o_group = jnp.concatenate(outs * parts, axis=0)
    part_of_row = lax.broadcasted_iota(jnp.int32, (V7X_SUBLANES, D_MODEL), 0) // nb
    rows = pl.ds(row0, V7X_SUBLANES)
    ohs_ref[rows, :] = jnp.where(part_of_row == sub, o_group, ohs_ref[rows, :])

    @pl.when(step == last)
    def _():
        os_ref[...] = xs_ref[...] + _dot(ohs_ref[...].astype(BF16), wo_ref[...])


def _attn(x, xs, g, wq, k, v, cache_k, cache_v, wo, layer, n_seq, seq_len):
    tm = ROW_TILE
    n_t = seq_len // tm
    nb, rem = divmod(xs.shape[0], n_seq * n_t)
    assert rem == 0 and V7X_SUBLANES % nb == 0
    row_spec = pl.BlockSpec((tm, D_MODEL), lambda b, i: (b * n_t + i, 0))
    kv_spec = pl.BlockSpec((None, N_MEM, D_MODEL), lambda b, i: (layer, b, 0))
    cache_spec = pl.BlockSpec((None, nb, N_MEM, X_HEADS, X_HEAD_DIM),
                              lambda b, i: (layer, b * n_t + i, 0, 0, 0))
    return pl.pallas_call(
        functools.partial(_attn_kernel, nb=nb),
        out_shape=(jax.ShapeDtypeStruct(x.shape, F32), jax.ShapeDtypeStruct(xs.shape, F32)),
        grid=(n_seq, n_t),
        in_specs=[row_spec,
                  _const_spec(xs.shape),
                  _layer_spec((1, D_MODEL), layer),
                  _layer_spec((D_MODEL, D_MODEL), layer),
                  kv_spec, kv_spec, cache_spec, cache_spec,
                  _layer_spec((D_MODEL, D_MODEL), layer)],
        out_specs=(row_spec, _const_spec(xs.shape)),
        scratch_shapes=[pltpu.VMEM((tm, D_MODEL), BF16),
                        pltpu.VMEM(xs.shape, F32), pltpu.VMEM(xs.shape, F32)],
        compiler_params=_params("arbitrary", "arbitrary"),
        name="attn",
    )(x, xs, g, wq, k, v, cache_k, cache_v, wo)


def _trunk(x, xs, n_seq, seq_len, mem_k, mem_v, cache_k, cache_v, state_re, state_im, w):
    n_samples = xs.shape[0]
    zero_state = jnp.zeros((n_seq, SSM_LANES), F32)
    p_re, p_im, s_re, s_im, s_v = [], [], [], [], []
    for i in range(DEPTH):
        j = i // 2
        x, xs = _ffn(x, xs, w['norm_ffn1'], w['ffn1_wg'], w['ffn1_wu'], w['ffn1_wd'], w['norm_final'],
                     i, False)
        if i % 2 == 0:
            x, xs, v_rows = _mixer_a(x, xs, w['norm_mix'], w['a_w_in'], w['a_ln_g'], w['a_ln_b'],
                                     w['a_w_s'], w['a_bias_full'], w['a_scale0'], w['a_bias0'],
                                     w['a_w_out'], i, j)
            s_v.append(v_rows)
        else:
            ssm_w = (w['norm_mix'], w['b_wb_re'], w['b_wb_im'], w['b_ab_re'], w['b_ab_im'],
                     w['b_wc_re'], w['b_wc_im'], w['b_d'], w['b_w_glu'])
            x, hr, hi = _ssm_mixer(x, *ssm_w, zero_state, zero_state, i, j, n_seq, seq_len)
            p_re.append(hr)
            p_im.append(hi)
            xs, hr, hi = _ssm_mixer(xs, *ssm_w, state_re[j], state_im[j], i, j, n_samples, 1)
            s_re.append(hr)
            s_im.append(hi)
        x, xs = _attn(x, xs, w['norm_x'], w['x_wq'], mem_k, mem_v, cache_k, cache_v, w['x_wo'],
                      i, n_seq, seq_len)
        x, xs = _ffn(x, xs, w['norm_ffn2'], w['ffn2_wg'], w['ffn2_wu'], w['ffn2_wd'], w['norm_final'],
                     i, i == DEPTH - 1)
    return x, xs, p_re, p_im, s_re, s_im, s_v


def kernel(x_prompt, x_sample, cache_mem_k, cache_mem_v, state_ssm_re, state_ssm_im, mem_prompt,
           norm_ffn1, ffn1_wg, ffn1_wu, ffn1_wd, norm_mix,
           a_w_in, a_ln_g, a_ln_b, a_w_s, a_b_s, a_w_out,
           b_a_re, b_a_im, b_log_dt, b_b_re, b_b_im, b_c_re, b_c_im, b_d, b_w_glu,
           norm_x, norm_mem, x_wq, x_wk, x_wv, x_wo,
           norm_ffn2, ffn2_wg, ffn2_wu, ffn2_wd, norm_final):
    batch, seq, _ = x_prompt.shape
    dec_batch, dec_seq, _ = x_sample.shape
    assert dec_seq == 1 and seq % ROW_TILE == 0 and ROW_TILE % CHUNK == 0
    n_a = a_w_in.shape[0]
    n_b = b_a_re.shape[0]

    def vec(p):
        return p.reshape(p.shape[0], 1, p.shape[-1])

    ab_re, ab_im, bb_re, bb_im = _ssm_discretize(
        b_a_re, b_a_im, b_log_dt, b_b_re.transpose(0, 1, 3, 2), b_b_im.transpose(0, 1, 3, 2))
    w = dict(
        norm_ffn1=vec(norm_ffn1), norm_ffn2=vec(norm_ffn2), norm_mix=vec(norm_mix), norm_x=vec(norm_x),
        norm_final=norm_final.reshape(1, D_MODEL),
        ffn1_wg=ffn1_wg, ffn1_wu=ffn1_wu, ffn1_wd=ffn1_wd,
        ffn2_wg=ffn2_wg, ffn2_wu=ffn2_wu, ffn2_wd=ffn2_wd,
        a_w_in=a_w_in, a_w_out=a_w_out,
        a_ln_g=vec(a_ln_g), a_ln_b=vec(a_ln_b), a_w_s=a_w_s,
        a_bias_full=jnp.repeat(jnp.swapaxes(a_b_s, 1, 2), A_HEAD_DIM, axis=-1),
        a_scale0=jnp.repeat(a_w_s[:, :, 0, 0], A_HEAD_DIM, axis=-1).reshape(n_a, 1, D_A),
        a_bias0=jnp.repeat(a_b_s[:, :, 0], A_HEAD_DIM, axis=-1).reshape(n_a, 1, D_A),
        b_wb_re=_block_diag(bb_re.astype(BF16), GROUP, STATE),
        b_wb_im=_block_diag(bb_im.astype(BF16), GROUP, STATE),
        b_ab_re=ab_re.reshape(n_b, 1, SSM_LANES), b_ab_im=ab_im.reshape(n_b, 1, SSM_LANES),
        b_wc_re=_block_diag(b_c_re.transpose(0, 1, 3, 2).astype(BF16), STATE, GROUP),
        b_wc_im=_block_diag(b_c_im.transpose(0, 1, 3, 2).astype(BF16), STATE, GROUP),
        b_d=b_d.reshape(n_b, 1, D_MODEL), b_w_glu=b_w_glu,
        x_wq=x_wq, x_wo=x_wo,
    )

    mem_k, mem_v, mem_kb, mem_vb = _mem_kv(mem_prompt, vec(norm_mem), x_wk, x_wv)
    y_prompt, y_sample, p_re, p_im, s_re, s_im, s_v = _trunk(
        x_prompt.reshape(batch * seq, D_MODEL), x_sample.reshape(dec_batch, D_MODEL), batch, seq,
        mem_kb, mem_vb, cache_mem_k, cache_mem_v,
        state_ssm_re.reshape(n_b, dec_batch, SSM_LANES),
        state_ssm_im.reshape(n_b, dec_batch, SSM_LANES), w)

    def states(parts, n):
        return jnp.stack(parts).reshape(n_b, n, N_GROUPS, STATE)

    return (y_prompt.reshape(batch, seq, D_MODEL),
            y_sample.reshape(dec_batch, 1, D_MODEL),
            mem_k, mem_v,
            states(p_re, batch), states(p_im, batch),
            states(s_re, dec_batch), states(s_im, dec_batch),
            jnp.stack(s_v).reshape(n_a, dec_batch, 1, D_A))
```

```python
import functools

import jax
import jax.numpy as jnp
from jax import lax
from jax.experimental import pallas as pl
from jax.experimental.pallas import tpu as pltpu

F32 = jnp.float32
BF16 = jnp.bfloat16

D_MODEL = 1024
DEPTH = 4
CHUNK = 128
D_A = 2 * D_MODEL
A_HEADS = 8
A_HEAD_DIM = D_A // A_HEADS
GROUP = 16
N_GROUPS = D_MODEL // GROUP
STATE = 64
SSM_LANES = N_GROUPS * STATE
N_MEM = 256
X_HEADS = 4
X_HEAD_DIM = D_MODEL // X_HEADS
D_FF = 2816
EPS = 1e-6

V7X_SUBLANES = 8
V7X_LANES = 128
V7X_VMEM_LIMIT_BYTES = 56 * 1024 * 1024

ROW_TILE = 512
FFN_FF_CHUNK = 768
SSM_TIME_TILE = 64
SSM_GROUP_BLOCK = 16
SSM_BLOCKS = N_GROUPS // SSM_GROUP_BLOCK
SCAN_LANES = 512
MEM_KV_SEQS = 2
CACHE_SLOTS = 3
CACHE_AHEAD = CACHE_SLOTS - 1
STREAM_DMA_PRIORITY = 1


def _params(*semantics):
    return pltpu.CompilerParams(dimension_semantics=semantics,
                                vmem_limit_bytes=V7X_VMEM_LIMIT_BYTES)


def _const_spec(shape):
    nd = len(shape)
    return pl.BlockSpec(shape, lambda *_: (0,) * nd, pipeline_mode=pl.Buffered(1))


def _layer_spec(shape, layer):
    nd = len(shape)
    return pl.BlockSpec((None,) + tuple(shape), lambda *_: (layer,) + (0,) * nd,
                        pipeline_mode=pl.Buffered(1))


def _rmsnorm(x, g):
    return x * lax.rsqrt(jnp.mean(x * x, axis=-1, keepdims=True) + EPS) * g


def _dot(a, b):
    return lax.dot_general(a, b, (((1,), (0,)), ((), ())), preferred_element_type=F32)


def _ffn_kernel(x_ref, xs_ref, g_ref, wg_hbm, wu_hbm, wd_hbm, gf_ref, o_ref, os_ref,
                wg_ref, wu_ref, wd_ref, sem, *, layer, final_norm):
    step = pl.program_id(0)
    chunks = [(c0, min(FFN_FF_CHUNK, D_FF - c0)) for c0 in range(0, D_FF, FFN_FF_CHUNK)]

    def copies(c):
        c0, w = chunks[c]
        return (pltpu.make_async_copy(wg_hbm.at[layer, :, pl.ds(c0, w)], wg_ref.at[:, pl.ds(c0, w)],
                                      sem.at[0, c]),
                pltpu.make_async_copy(wu_hbm.at[layer, :, pl.ds(c0, w)], wu_ref.at[:, pl.ds(c0, w)],
                                      sem.at[1, c]),
                pltpu.make_async_copy(wd_hbm.at[layer, pl.ds(c0, w), :], wd_ref.at[pl.ds(c0, w), :],
                                      sem.at[2, c]))

    def wait_chunk(c):
        for cp in copies(c):
            cp.wait()

    def ffn_rows(x, before_chunk=None):
        h = _rmsnorm(x, g_ref[...]).astype(BF16)
        down = None
        for c, (c0, w) in enumerate(chunks):
            if before_chunk is not None:
                before_chunk(c)
            cols = slice(c0, c0 + w)
            gate = _dot(h, wg_ref[:, cols])
            up = _dot(h, wu_ref[:, cols])
            act = (jax.nn.silu(gate) * up).astype(BF16)
            part = _dot(act, wd_ref[cols, :])
            down = part if down is None else down + part
        y = x + 0.5 * down
        return _rmsnorm(y, gf_ref[...]) if final_norm else y

    @pl.when(step == 0)
    def _():
        for c in range(len(chunks)):
            for cp in copies(c):
                cp.start(priority=0 if c == 0 else STREAM_DMA_PRIORITY)
        o_ref[...] = ffn_rows(x_ref[...], wait_chunk)

    @pl.when(step != 0)
    def _():
        o_ref[...] = ffn_rows(x_ref[...])

    @pl.when(step == pl.num_programs(0) - 1)
    def _():
        os_ref[...] = ffn_rows(xs_ref[...])


def _ffn(x, xs, g, wg, wu, wd, g_final, layer, final_norm):
    rows = x.shape[0]
    tm = ROW_TILE
    assert rows // tm > 1
    row_spec = pl.BlockSpec((tm, D_MODEL), lambda i: (i, 0))
    hbm_spec = pl.BlockSpec(memory_space=pl.ANY)
    n_chunks = pl.cdiv(D_FF, FFN_FF_CHUNK)
    return pl.pallas_call(
        functools.partial(_ffn_kernel, layer=layer, final_norm=final_norm),
        out_shape=(jax.ShapeDtypeStruct(x.shape, F32), jax.ShapeDtypeStruct(xs.shape, F32)),
        grid=(rows // tm,),
        in_specs=[row_spec,
                  _const_spec(xs.shape),
                  _layer_spec((1, D_MODEL), layer),
                  hbm_spec, hbm_spec, hbm_spec,
                  _const_spec((1, D_MODEL))],
        out_specs=(row_spec, _const_spec(xs.shape)),
        scratch_shapes=[pltpu.VMEM((D_MODEL, D_FF), F32), pltpu.VMEM((D_MODEL, D_FF), F32),
                        pltpu.VMEM((D_FF, D_MODEL), F32),
                        pltpu.SemaphoreType.DMA((3, n_chunks))],
        compiler_params=_params("arbitrary"),
        name="ffn",
    )(x, xs, g, wg, wu, wd, g_final)


def _gated_unit_inputs(x_ref, g_ref, w_in_ref, ln_g_ref, ln_b_ref):
    h = _rmsnorm(x_ref[...], g_ref[...]).astype(BF16)
    uv = jax.nn.gelu(_dot(h, w_in_ref[...]))
    u = uv[:, :D_A]
    v = uv[:, D_A:]
    mu = jnp.mean(v, axis=-1, keepdims=True)
    vc = v - mu
    var = jnp.mean(vc * vc, axis=-1, keepdims=True)
    v = vc * lax.rsqrt(var + EPS) * ln_g_ref[...] + ln_b_ref[...]
    return u, v


def _mixer_a_kernel(x_ref, xs_ref, g_ref, w_in_ref, ln_g_ref, ln_b_ref, ws_ref, bias_ref,
                    scale0_ref, bias0_ref, w_out_ref, o_ref, os_ref, vs_ref, gated_ref):
    u, v = _gated_unit_inputs(x_ref, g_ref, w_in_ref, ln_g_ref, ln_b_ref)
    vb = v.astype(BF16)
    t_idx = lax.broadcasted_iota(jnp.int32, (CHUNK, CHUNK), 0)
    s_idx = lax.broadcasted_iota(jnp.int32, (CHUNK, CHUNK), 1)
    causal = t_idx >= s_idx
    for head in range(A_HEADS):
        cols = slice(head * A_HEAD_DIM, (head + 1) * A_HEAD_DIM)
        ws = jnp.where(causal, ws_ref[head], 0.0).astype(BF16)
        bias = bias_ref[:, cols]
        for c in range(x_ref.shape[0] // CHUNK):
            rows = slice(c * CHUNK, (c + 1) * CHUNK)
            mixed = _dot(ws, vb[rows, cols]) + bias
            gated_ref[rows, cols] = (u[rows, cols] * mixed).astype(BF16)
    o_ref[...] = x_ref[...] + _dot(gated_ref[...], w_out_ref[...])

    @pl.when(pl.program_id(0) == pl.num_programs(0) - 1)
    def _():
        us, vs = _gated_unit_inputs(xs_ref, g_ref, w_in_ref, ln_g_ref, ln_b_ref)
        vs_ref[...] = vs
        mixed = vs * scale0_ref[...] + bias0_ref[...]
        os_ref[...] = xs_ref[...] + _dot((us * mixed).astype(BF16), w_out_ref[...])


def _mixer_a(x, xs, g, w_in, ln_g, ln_b, w_s, bias_full, scale0, bias0, w_out, layer, j):
    rows = x.shape[0]
    tm = ROW_TILE
    row_spec = pl.BlockSpec((tm, D_MODEL), lambda i: (i, 0))
    vs_shape = (xs.shape[0], D_A)
    return pl.pallas_call(
        _mixer_a_kernel,
        out_shape=(jax.ShapeDtypeStruct(x.shape, F32), jax.ShapeDtypeStruct(xs.shape, F32),
                   jax.ShapeDtypeStruct(vs_shape, F32)),
        grid=(rows // tm,),
        in_specs=[row_spec,
                  _const_spec(xs.shape),
                  _layer_spec((1, D_MODEL), layer),
                  _layer_spec((D_MODEL, 2 * D_A), j),
                  _layer_spec((1, D_A), j),
                  _layer_spec((1, D_A), j),
                  _layer_spec((A_HEADS, CHUNK, CHUNK), j),
                  _layer_spec((CHUNK, D_A), j),
                  _layer_spec((1, D_A), j),
                  _layer_spec((1, D_A), j),
                  _layer_spec((D_A, D_MODEL), j)],
        out_specs=(row_spec, _const_spec(xs.shape), _const_spec(vs_shape)),
        scratch_shapes=[pltpu.VMEM((tm, D_A), BF16)],
        compiler_params=_params("arbitrary"),
        name="mixer_a",
    )(x, xs, g, w_in, ln_g, ln_b, w_s, bias_full, scale0, bias0, w_out)


def _ssm_discretize_kernel(a_re_ref, a_im_ref, log_dt_ref, b_re_ref, b_im_ref,
                           ab_re_ref, ab_im_ref, bb_re_ref, bb_im_ref):
    a_re = a_re_ref[...]
    a_im = a_im_ref[...]
    dt = jnp.exp(log_dt_ref[...])
    mag = jnp.exp(a_re * dt)
    ab_re = mag * jnp.cos(a_im * dt)
    ab_im = mag * jnp.sin(a_im * dt)
    den = a_re * a_re + a_im * a_im
    nr = ab_re - 1.0
    ni = ab_im
    f_re = (nr * a_re + ni * a_im) / den
    f_im = (ni * a_re - nr * a_im) / den
    b_re = b_re_ref[...]
    b_im = b_im_ref[...]
    ab_re_ref[...] = ab_re
    ab_im_ref[...] = ab_im
    bb_re_ref[...] = f_re * b_re - f_im * b_im
    bb_im_ref[...] = f_re * b_im + f_im * b_re


def _ssm_discretize(a_re, a_im, log_dt, b_re_t, b_im_t):
    n_layers = a_re.shape[0]
    ab_shape = jax.ShapeDtypeStruct((n_layers, N_GROUPS, 1, STATE), F32)
    bb_shape = jax.ShapeDtypeStruct((n_layers, N_GROUPS, GROUP, STATE), F32)
    return pl.pallas_call(
        _ssm_discretize_kernel,
        out_shape=(ab_shape, ab_shape, bb_shape, bb_shape),
        name="ssm_discretize",
    )(a_re.reshape(ab_shape.shape), a_im.reshape(ab_shape.shape),
      log_dt.reshape(n_layers, N_GROUPS, 1, 1), b_re_t, b_im_t)


def _ssm_kernel(x_ref, g_ref, wb_re_ref, wb_im_ref, ab_re_ref, ab_im_ref, wc_re_ref, wc_im_ref, d_ref,
                wglu_ref, h0_re_ref, h0_im_ref, o_ref, hf_re_ref, hf_im_ref,
                s_re_ref, s_im_ref, st_re_ref, st_im_ref, tm_ref, abb_re_ref, abb_im_ref,
                *, n_batch, n_time):
    step = pl.program_id(0)
    blk = SSM_GROUP_BLOCK * STATE
    cblk = SSM_GROUP_BLOCK * GROUP
    lane_tiles = [slice(j * V7X_LANES, (j + 1) * V7X_LANES) for j in range(D_MODEL // V7X_LANES)]

    @pl.when(step == 0)
    def _():
        st_re_ref[...] = h0_re_ref[...]
        st_im_ref[...] = h0_im_ref[...]
        abb_re_ref[...] = jnp.broadcast_to(ab_re_ref[...], abb_re_ref.shape)
        abb_im_ref[...] = jnp.broadcast_to(ab_im_ref[...], abb_im_ref.shape)

    if n_time == 1:
        u = _rmsnorm(x_ref[...], g_ref[...])
    else:
        for b in range(n_batch):
            ub_rows = _rmsnorm(x_ref[b], g_ref[...])
            for j, cols in enumerate(lane_tiles):
                tm_ref[j, pl.ds(b, n_time, stride=n_batch), :] = ub_rows[:, cols]
        u = jnp.concatenate([tm_ref[j] for j in range(len(lane_tiles))], axis=1)
    ub = u.astype(BF16)

    def project_in(k):
        uk = ub[:, k * cblk:(k + 1) * cblk]
        s_re_ref[:, k * blk:(k + 1) * blk] = _dot(uk, wb_re_ref[k])
        s_im_ref[:, k * blk:(k + 1) * blk] = _dot(uk, wb_im_ref[k])

    def scan(c):
        lanes = slice(c * SCAN_LANES, (c + 1) * SCAN_LANES)
        ar = abb_re_ref[:, lanes]
        ai = abb_im_ref[:, lanes]
        for s in range(n_batch // V7X_SUBLANES):
            rows0 = slice(s * V7X_SUBLANES, (s + 1) * V7X_SUBLANES)
            hr = st_re_ref[rows0, lanes]
            hi = st_im_ref[rows0, lanes]
            for t in range(n_time):
                rows = slice(t * n_batch + s * V7X_SUBLANES, t * n_batch + (s + 1) * V7X_SUBLANES)
                hr, hi = (ar * hr - ai * hi + s_re_ref[rows, lanes],
                          ar * hi + ai * hr + s_im_ref[rows, lanes])
                s_re_ref[rows, lanes] = hr
                s_im_ref[rows, lanes] = hi
            st_re_ref[rows0, lanes] = hr
            st_im_ref[rows0, lanes] = hi

    def project_out(k):
        hr = s_re_ref[:, k * blk:(k + 1) * blk].astype(BF16)
        hi = s_im_ref[:, k * blk:(k + 1) * blk].astype(BF16)
        return _dot(hr, wc_re_ref[k]) - _dot(hi, wc_im_ref[k])

    ys = []
    if n_time == 1:
        for k in range(SSM_BLOCKS):
            project_in(k)
        ar = ab_re_ref[...]
        ai = ab_im_ref[...]
        hr0 = st_re_ref[...]
        hi0 = st_im_ref[...]
        hr = ar * hr0 - ai * hi0 + s_re_ref[...]
        hi = ar * hi0 + ai * hr0 + s_im_ref[...]
        s_re_ref[...] = hr
        s_im_ref[...] = hi
        st_re_ref[...] = hr
        st_im_ref[...] = hi
        for k in range(SSM_BLOCKS):
            ys.append(project_out(k))
    else:
        chunks_per_block = blk // SCAN_LANES
        project_in(0)
        for k in range(SSM_BLOCKS):
            if k + 1 < SSM_BLOCKS:
                project_in(k + 1)
            for c in range(chunks_per_block):
                scan(k * chunks_per_block + c)
            ys.append(project_out(k))
    y = jnp.concatenate(ys, axis=-1) + d_ref[...] * u
    z = jax.nn.gelu(y).astype(BF16)
    vg = _dot(z, wglu_ref[...])
    out = vg[:, :D_MODEL] * jax.nn.sigmoid(vg[:, D_MODEL:])
    if n_time == 1:
        o_ref[...] = x_ref[...] + out
    else:
        for j, cols in enumerate(lane_tiles):
            tm_ref[j] = out[:, cols]
        for b in range(n_batch):
            rows = pl.ds(b, n_time, stride=n_batch)
            out_b = jnp.concatenate([tm_ref[j, rows, :] for j in range(len(lane_tiles))], axis=1)
            o_ref[b] = x_ref[b] + out_b

    @pl.when(step == pl.num_programs(0) - 1)
    def _():
        hf_re_ref[...] = st_re_ref[...]
        hf_im_ref[...] = st_im_ref[...]


def _ssm_mixer(x, g, wb_re, wb_im, ab_re, ab_im, wc_re, wc_im, d_skip, w_glu, h0_re, h0_im,
               layer, j, n_batch, n_time_total):
    n_time = min(SSM_TIME_TILE, n_time_total)
    tile = n_time * n_batch
    if n_time_total == 1:
        x_in = x
        row_spec = _const_spec((n_batch, D_MODEL))
    else:
        x_in = x.reshape(n_batch, n_time_total, D_MODEL)
        row_spec = pl.BlockSpec((n_batch, n_time, D_MODEL), lambda i: (0, i, 0))
    state_spec = _const_spec((n_batch, SSM_LANES))
    state_shape = jax.ShapeDtypeStruct((n_batch, SSM_LANES), F32)
    blk = SSM_GROUP_BLOCK * STATE
    cblk = SSM_GROUP_BLOCK * GROUP
    out, h_re, h_im = pl.pallas_call(
        functools.partial(_ssm_kernel, n_batch=n_batch, n_time=n_time),
        out_shape=(jax.ShapeDtypeStruct(x_in.shape, F32), state_shape, state_shape),
        grid=(n_time_total // n_time,),
        in_specs=[row_spec,
                  _layer_spec((1, D_MODEL), layer),
                  _layer_spec((SSM_BLOCKS, cblk, blk), j),
                  _layer_spec((SSM_BLOCKS, cblk, blk), j),
                  _layer_spec((1, SSM_LANES), j),
                  _layer_spec((1, SSM_LANES), j),
                  _layer_spec((SSM_BLOCKS, blk, cblk), j),
                  _layer_spec((SSM_BLOCKS, blk, cblk), j),
                  _layer_spec((1, D_MODEL), j),
                  _layer_spec((D_MODEL, 2 * D_MODEL), j),
                  state_spec, state_spec],
        out_specs=(row_spec, state_spec, state_spec),
        scratch_shapes=[pltpu.VMEM((tile, SSM_LANES), F32), pltpu.VMEM((tile, SSM_LANES), F32),
                        pltpu.VMEM((n_batch, SSM_LANES), F32), pltpu.VMEM((n_batch, SSM_LANES), F32),
                        pltpu.VMEM((D_MODEL // V7X_LANES, tile, V7X_LANES), F32),
                        pltpu.VMEM((V7X_SUBLANES, SSM_LANES), F32),
                        pltpu.VMEM((V7X_SUBLANES, SSM_LANES), F32)],
        compiler_params=_params("arbitrary"),
        name="ssm_mixer",
    )(x_in, g, wb_re, wb_im, ab_re, ab_im, wc_re, wc_im, d_skip, w_glu, h0_re, h0_im)
    return out.reshape(x.shape), h_re, h_im


def _block_diag(w, rows_inner, cols_inner):
    n_layers = w.shape[0]
    rows, cols = SSM_GROUP_BLOCK * rows_inner, SSM_GROUP_BLOCK * cols_inner
    tiled = jnp.tile(w.reshape(n_layers, SSM_BLOCKS, rows, cols_inner), (1, 1, 1, SSM_GROUP_BLOCK))
    row_group = lax.broadcasted_iota(jnp.int32, (rows, cols), 0) // rows_inner
    col_group = lax.broadcasted_iota(jnp.int32, (rows, cols), 1) // cols_inner
    return jnp.where(row_group == col_group, tiled, jnp.zeros_like(tiled))


def _mem_kv_kernel(m_ref, g_ref, wk_ref, wv_ref, k_ref, v_ref, kb_ref, vb_ref):
    m = _rmsnorm(m_ref[...], g_ref[...]).astype(BF16)
    k = _dot(m, wk_ref[...])
    v = _dot(m, wv_ref[...])
    kb_ref[...] = k.astype(BF16)
    vb_ref[...] = v.astype(BF16)
    for s in range(MEM_KV_SEQS):
        rows = slice(s * N_MEM, (s + 1) * N_MEM)
        for h in range(X_HEADS):
            cols = slice(h * X_HEAD_DIM, (h + 1) * X_HEAD_DIM)
            k_ref[s, :, h, :] = k[rows, cols]
            v_ref[s, :, h, :] = v[rows, cols]


def _mem_kv(mem, g, wk, wv):
    n_seq = mem.shape[0]
    tm = MEM_KV_SEQS * N_MEM
    out = jax.ShapeDtypeStruct((DEPTH, n_seq, N_MEM, X_HEADS, X_HEAD_DIM), F32)
    out_b = jax.ShapeDtypeStruct((DEPTH, n_seq * N_MEM, D_MODEL), BF16)
    w_spec = pl.BlockSpec((None, D_MODEL, D_MODEL), lambda l, i: (l, 0, 0))
    o_spec = pl.BlockSpec((None, MEM_KV_SEQS, N_MEM, X_HEADS, X_HEAD_DIM), lambda l, i: (l, i, 0, 0, 0))
    ob_spec = pl.BlockSpec((None, tm, D_MODEL), lambda l, i: (l, i, 0))
    return pl.pallas_call(
        _mem_kv_kernel,
        out_shape=(out, out, out_b, out_b),
        grid=(DEPTH, n_seq // MEM_KV_SEQS),
        in_specs=[pl.BlockSpec((tm, D_MODEL), lambda l, i: (i, 0)),
                  pl.BlockSpec((None, 1, D_MODEL), lambda l, i: (l, 0, 0)),
                  w_spec, w_spec],
        out_specs=(o_spec, o_spec, ob_spec, ob_spec),
        compiler_params=_params("parallel", "parallel"),
        name="mem_kv",
    )(mem.reshape(n_seq * N_MEM, D_MODEL), g, wk, wv)


def _softmax_rows(s):
    e = jnp.exp(s - jnp.max(s, axis=-1, keepdims=True))
    return e / jnp.sum(e, axis=-1, keepdims=True)


def _attn_kernel(x_ref, xs_ref, g_ref, wq_ref, k_ref, v_ref, ck_hbm, cv_hbm, wo_ref, o_ref, os_ref,
                 oh_ref, qs_ref, ohs_ref, ck_buf, cv_buf, sem, *, nb, layer, n_steps):
    b, t = pl.program_id(0), pl.program_id(1)
    step = b * pl.num_programs(1) + t
    last = n_steps - 1
    scale = X_HEAD_DIM ** -0.5
    head_cols = [slice(h * X_HEAD_DIM, (h + 1) * X_HEAD_DIM) for h in range(X_HEADS)]

    def cache_copies(s, slot):
        rows = pl.ds(s * nb, nb)
        return (pltpu.make_async_copy(ck_hbm.at[layer, rows], ck_buf.at[slot], sem.at[0, slot]),
                pltpu.make_async_copy(cv_hbm.at[layer, rows], cv_buf.at[slot], sem.at[1, slot]))

    @pl.when(step == 0)
    def _():
        for s in range(min(CACHE_AHEAD, n_steps)):
            for cp in cache_copies(s, s):
                cp.start(priority=STREAM_DMA_PRIORITY)
        hs = _rmsnorm(xs_ref[...], g_ref[...]).astype(BF16)
        qs_ref[...] = _dot(hs, wq_ref[...])
        ohs_ref[...] = jnp.zeros_like(ohs_ref)

    @pl.when(step + CACHE_AHEAD < n_steps)
    def _():
        for cp in cache_copies(step + CACHE_AHEAD, (step + CACHE_AHEAD) % CACHE_SLOTS):
            cp.start(priority=STREAM_DMA_PRIORITY)

    slot = step % CACHE_SLOTS
    for cp in cache_copies(step, slot):
        cp.wait()

    x = x_ref[...]
    h = _rmsnorm(x, g_ref[...]).astype(BF16)
    q = _dot(h, wq_ref[...])
    qb = q.astype(BF16)
    s_all = jnp.concatenate(
        [lax.dot_general(qb[:, cols], k_ref[:, cols], (((1,), (1,)), ((), ())),
                         preferred_element_type=F32) for cols in head_cols], axis=0) * scale
    p_all = _softmax_rows(s_all).astype(BF16)
    tm = x_ref.shape[0]
    for i, cols in enumerate(head_cols):
        oh_ref[:, cols] = _dot(p_all[i * tm:(i + 1) * tm, :], v_ref[:, cols]).astype(BF16)
    o_ref[...] = x + _dot(oh_ref[...], wo_ref[...])

    parts = V7X_SUBLANES // nb
    sub = step % parts
    row0 = pl.multiple_of((step // parts) * V7X_SUBLANES, V7X_SUBLANES)
    q_group = qs_ref[pl.ds(row0, V7X_SUBLANES), :]
    q_rows = q_group[:nb, :]
    for c in range(1, parts):
        q_rows = jnp.where(sub == c, q_group[c * nb:(c + 1) * nb, :], q_rows)
    n_rows = N_MEM * X_HEADS
    head_of_lane = lax.broadcasted_iota(jnp.int32, (V7X_SUBLANES, n_rows), 1) % X_HEADS
    head_of_row = lax.broadcasted_iota(jnp.int32, (V7X_SUBLANES, n_rows), 0) % X_HEADS
    own_head = head_of_lane == head_of_row
    scores = []
    for r in range(nb):
        qr = q_rows[r:r + 1, :]
        qh = jnp.concatenate([qr[:, c] for c in head_cols] * (V7X_SUBLANES // X_HEADS), axis=0)
        k2 = ck_buf[slot, r].reshape(n_rows, X_HEAD_DIM).astype(BF16)
        scores.append(lax.dot_general(qh.astype(BF16), k2, (((1,), (1,)), ((), ())),
                                      preferred_element_type=F32))
    sc = jnp.concatenate(scores, axis=0) * scale
    own = jnp.concatenate([own_head] * nb, axis=0)
    p_all = _softmax_rows(jnp.where(own, sc, -jnp.inf)).astype(BF16)
    outs = []
    for r in range(nb):
        v2 = cv_buf[slot, r].reshape(n_rows, X_HEAD_DIM).astype(BF16)
        pv = _dot(p_all[r * V7X_SUBLANES:(r + 1) * V7X_SUBLANES, :], v2)
        outs.append(jnp.concatenate([pv[hd:hd + 1, :] for hd in range(X_HEADS)], axis=1))
    o_group = jnp.concatenate(outs * parts, axis=0)
    part_of_row = lax.broadcasted_iota(jnp.int32, (V7X_SUBLANES, D_MODEL), 0) // nb
    rows = pl.ds(row0, V7X_SUBLANES)
    ohs_ref[rows, :] = jnp.where(part_of_row == sub, o_group, ohs_ref[rows, :])

    @pl.when(step == last)
    def _():
        os_ref[...] = xs_ref[...] + _dot(ohs_ref[...].astype(BF16), wo_ref[...])


def _attn(x, xs, g, wq, k, v, cache_k, cache_v, wo, layer, n_seq, seq_len):
    tm = ROW_TILE
    n_t = seq_len // tm
    nb, rem = divmod(xs.shape[0], n_seq * n_t)
    assert rem == 0 and V7X_SUBLANES % nb == 0
    row_spec = pl.BlockSpec((tm, D_MODEL), lambda b, i: (b * n_t + i, 0))
    kv_spec = pl.BlockSpec((None, N_MEM, D_MODEL), lambda b, i: (layer, b, 0))
    cache_spec = pl.BlockSpec(memory_space=pl.ANY)
    cache_slot = pltpu.VMEM((CACHE_SLOTS, nb, N_MEM, X_HEADS, X_HEAD_DIM), F32)
    return pl.pallas_call(
        functools.partial(_attn_kernel, nb=nb, layer=layer, n_steps=n_seq * n_t),
        out_shape=(jax.ShapeDtypeStruct(x.shape, F32), jax.ShapeDtypeStruct(xs.shape, F32)),
        grid=(n_seq, n_t),
        in_specs=[row_spec,
                  _const_spec(xs.shape),
                  _layer_spec((1, D_MODEL), layer),
                  _layer_spec((D_MODEL, D_MODEL), layer),
                  kv_spec, kv_spec, cache_spec, cache_spec,
                  _layer_spec((D_MODEL, D_MODEL), layer)],
        out_specs=(row_spec, _const_spec(xs.shape)),
        scratch_shapes=[pltpu.VMEM((tm, D_MODEL), BF16),
                        pltpu.VMEM(xs.shape, F32), pltpu.VMEM(xs.shape, F32),
                        cache_slot, cache_slot, pltpu.SemaphoreType.DMA((2, CACHE_SLOTS))],
        compiler_params=_params("arbitrary", "arbitrary"),
        name="attn",
    )(x, xs, g, wq, k, v, cache_k, cache_v, wo)


def _trunk(x, xs, n_seq, seq_len, mem_k, mem_v, cache_k, cache_v, state_re, state_im, w):
    n_samples = xs.shape[0]
    zero_state = jnp.zeros((n_seq, SSM_LANES), F32)
    p_re, p_im, s_re, s_im, s_v = [], [], [], [], []
    for i in range(DEPTH):
        j = i // 2
        x, xs = _ffn(x, xs, w['norm_ffn1'], w['ffn1_wg'], w['ffn1_wu'], w['ffn1_wd'], w['norm_final'],
                     i, False)
        if i % 2 == 0:
            x, xs, v_rows = _mixer_a(x, xs, w['norm_mix'], w['a_w_in'], w['a_ln_g'], w['a_ln_b'],
                                     w['a_w_s'], w['a_bias_full'], w['a_scale0'], w['a_bias0'],
                                     w['a_w_out'], i, j)
            s_v.append(v_rows)
        else:
            ssm_w = (w['norm_mix'], w['b_wb_re'], w['b_wb_im'], w['b_ab_re'], w['b_ab_im'],
                     w['b_wc_re'], w['b_wc_im'], w['b_d'], w['b_w_glu'])
            x, hr, hi = _ssm_mixer(x, *ssm_w, zero_state, zero_state, i, j, n_seq, seq_len)
            p_re.append(hr)
            p_im.append(hi)
            xs, hr, hi = _ssm_mixer(xs, *ssm_w, state_re[j], state_im[j], i, j, n_samples, 1)
            s_re.append(hr)
            s_im.append(hi)
        x, xs = _attn(x, xs, w['norm_x'], w['x_wq'], mem_k, mem_v, cache_k, cache_v, w['x_wo'],
                      i, n_seq, seq_len)
        x, xs = _ffn(x, xs, w['norm_ffn2'], w['ffn2_wg'], w['ffn2_wu'], w['ffn2_wd'], w['norm_final'],
                     i, i == DEPTH - 1)
    return x, xs, p_re, p_im, s_re, s_im, s_v


def kernel(x_prompt, x_sample, cache_mem_k, cache_mem_v, state_ssm_re, state_ssm_im, mem_prompt,
           norm_ffn1, ffn1_wg, ffn1_wu, ffn1_wd, norm_mix,
           a_w_in, a_ln_g, a_ln_b, a_w_s, a_b_s, a_w_out,
           b_a_re, b_a_im, b_log_dt, b_b_re, b_b_im, b_c_re, b_c_im, b_d, b_w_glu,
           norm_x, norm_mem, x_wq, x_wk, x_wv, x_wo,
           norm_ffn2, ffn2_wg, ffn2_wu, ffn2_wd, norm_final):
    batch, seq, _ = x_prompt.shape
    dec_batch, dec_seq, _ = x_sample.shape
    assert dec_seq == 1 and seq % ROW_TILE == 0 and ROW_TILE % CHUNK == 0
    n_a = a_w_in.shape[0]
    n_b = b_a_re.shape[0]

    def vec(p):
        return p.reshape(p.shape[0], 1, p.shape[-1])

    ab_re, ab_im, bb_re, bb_im = _ssm_discretize(
        b_a_re, b_a_im, b_log_dt, b_b_re.transpose(0, 1, 3, 2), b_b_im.transpose(0, 1, 3, 2))
    w = dict(
        norm_ffn1=vec(norm_ffn1), norm_ffn2=vec(norm_ffn2), norm_mix=vec(norm_mix), norm_x=vec(norm_x),
        norm_final=norm_final.reshape(1, D_MODEL),
        ffn1_wg=ffn1_wg, ffn1_wu=ffn1_wu, ffn1_wd=ffn1_wd,
        ffn2_wg=ffn2_wg, ffn2_wu=ffn2_wu, ffn2_wd=ffn2_wd,
        a_w_in=a_w_in, a_w_out=a_w_out,
        a_ln_g=vec(a_ln_g), a_ln_b=vec(a_ln_b), a_w_s=a_w_s,
        a_bias_full=jnp.repeat(jnp.swapaxes(a_b_s, 1, 2), A_HEAD_DIM, axis=-1),
        a_scale0=jnp.repeat(a_w_s[:, :, 0, 0], A_HEAD_DIM, axis=-1).reshape(n_a, 1, D_A),
        a_bias0=jnp.repeat(a_b_s[:, :, 0], A_HEAD_DIM, axis=-1).reshape(n_a, 1, D_A),
        b_wb_re=_block_diag(bb_re.astype(BF16), GROUP, STATE),
        b_wb_im=_block_diag(bb_im.astype(BF16), GROUP, STATE),
        b_ab_re=ab_re.reshape(n_b, 1, SSM_LANES), b_ab_im=ab_im.reshape(n_b, 1, SSM_LANES),
        b_wc_re=_block_diag(b_c_re.transpose(0, 1, 3, 2).astype(BF16), STATE, GROUP),
        b_wc_im=_block_diag(b_c_im.transpose(0, 1, 3, 2).astype(BF16), STATE, GROUP),
        b_d=b_d.reshape(n_b, 1, D_MODEL), b_w_glu=b_w_glu,
        x_wq=x_wq, x_wo=x_wo,
    )

    mem_k, mem_v, mem_kb, mem_vb = _mem_kv(mem_prompt, vec(norm_mem), x_wk, x_wv)
    y_prompt, y_sample, p_re, p_im, s_re, s_im, s_v = _trunk(
        x_prompt.reshape(batch * seq, D_MODEL), x_sample.reshape(dec_batch, D_MODEL), batch, seq,
        mem_kb, mem_vb, cache_mem_k, cache_mem_v,
        state_ssm_re.reshape(n_b, dec_batch, SSM_LANES),
        state_ssm_im.reshape(n_b, dec_batch, SSM_LANES), w)

    def states(parts, n):
        return jnp.stack(parts).reshape(n_b, n, N_GROUPS, STATE)

    return (y_prompt.reshape(batch, seq, D_MODEL),
            y_sample.reshape(dec_batch, 1, D_MODEL),
            mem_k, mem_v,
            states(p_re, batch), states(p_im, batch),
            states(s_re, dec_batch), states(s_im, dec_batch),
            jnp.stack(s_v).reshape(n_a, dec_batch, 1, D_A))
```
